```python
import jax, jax.numpy as jnp
from jax import lax
import numpy as np

D_MODEL = 4096
BATCH = 2
SEQ = 4096
DEPTH = 2

GRID_W = 64
CTX_LEN = 256
D_POOL = D_MODEL // 2
POOL_WINDOWS = (2, 4, 8, 16)
N_POOL_GROUPS = len(POOL_WINDOWS)
D_POOL_GROUP = D_POOL // N_POOL_GROUPS
HEAD_DIM = 128
D_ATTN = D_MODEL // 2
N_HEADS = D_ATTN // HEAD_DIM
NA_KH_MAX = 8
NA_KW = 16
NA_QB_W = 16
NA_NCB = GRID_W // NA_QB_W
NA_CW = 2 * NA_KW
D_IN = 2 * D_POOL + 4 * D_ATTN + 2 * D_MODEL
RMS_EPS = 1e-6
NEG_INF = -1e30

kernel_name = "hybrid_pool_natten_prefix_dit"


def _rmsnorm(x, g):
    xf = x.astype(jnp.float32)
    y = xf * lax.rsqrt(jnp.mean(xf * xf, axis=-1, keepdims=True) + RMS_EPS)
    return (y * g.astype(jnp.float32)).astype(x.dtype)


def _modulate(x, g, shift, scale):
    return _rmsnorm(x, g) * (1 + scale) + shift


def _split_proj(p):
    sizes = (D_POOL, D_POOL, D_ATTN, D_ATTN, D_ATTN, D_ATTN, D_MODEL, D_MODEL)
    return jnp.split(p, [int(i) for i in np.cumsum(sizes)[:-1]], axis=-1)


def _heads(a):
    b, l, _ = a.shape
    return a.reshape(b, l, N_HEADS, HEAD_DIM)


def _pool_mix(u, w_pool, s_pool):
    b, l, _ = u.shape
    uf = u.astype(jnp.float32)
    csum = jnp.concatenate([jnp.zeros((b, 1, D_POOL), jnp.float32), jnp.cumsum(uf, axis=1)], axis=1)
    t = jnp.arange(l)
    outs = []
    for gi, w in enumerate(POOL_WINDOWS):
        lo = jnp.clip(t - w // 2, 0, l - 1)
        hi = jnp.clip(t - w // 2 + w - 1, 0, l - 1)
        cols = slice(gi * D_POOL_GROUP, (gi + 1) * D_POOL_GROUP)
        window_sum = csum[:, hi + 1, cols] - csum[:, lo, cols]
        count = (hi - lo + 1).astype(jnp.float32)[None, :, None]
        outs.append(window_sum / count - uf[:, :, cols])
    p = jnp.stack(outs, axis=2).astype(u.dtype)
    p = jnp.einsum('blgc,gcd->blgd', p, w_pool).reshape(b, l, D_POOL)
    return p * s_pool


def _na_column_tables():
    j = np.arange(NA_NCB)
    col_start = np.clip(j * NA_QB_W - NA_KW // 2, 0, GRID_W - NA_CW)
    key_col = col_start[:, None] + np.arange(NA_CW)[None, :]
    q_col = j[:, None] * NA_QB_W + np.arange(NA_QB_W)[None, :]
    win_start = np.clip(q_col - NA_KW // 2, 0, GRID_W - NA_KW)
    kc = key_col[:, None, :]
    valid = (kc >= win_start[:, :, None]) & (kc < win_start[:, :, None] + NA_KW)
    dc_idx = np.clip(kc - q_col[:, :, None] + NA_KW - 1, 0, 2 * NA_KW - 2)
    return key_col, valid, dc_idx


def _neighbourhood_attention(q, k, v, kc, vc, rpb):
    b, l = q.shape[:2]
    rows = l // GRID_W
    kh = min(NA_KH_MAX, rows)

    def grid(a):
        return a.transpose(0, 2, 1, 3).reshape(b, N_HEADS, rows, GRID_W, HEAD_DIM)

    qg, kg, vg = grid(q * HEAD_DIM ** -0.5), grid(k), grid(v)
    kct, vct = kc.transpose(0, 2, 1, 3), vc.transpose(0, 2, 1, 3)
    key_col, valid, dc_idx = _na_column_tables()
    n_loc = kh * NA_CW

    def row(r):
        rs = jnp.clip(r - kh // 2, 0, rows - kh)
        q_r = lax.dynamic_index_in_dim(qg, r, axis=2, keepdims=False)
        q_r = q_r.reshape(b, N_HEADS, NA_NCB, NA_QB_W, HEAD_DIM)
        k_rows = lax.dynamic_slice_in_dim(kg, rs, kh, axis=2)
        v_rows = lax.dynamic_slice_in_dim(vg, rs, kh, axis=2)
        k_blk = k_rows[:, :, :, key_col]
        v_blk = v_rows[:, :, :, key_col]
        dr_idx = rs + jnp.arange(kh) - r + NA_KH_MAX - 1
        bias = rpb[:, dr_idx[None, None, :, None], dc_idx[:, :, None, :]]
        s_loc = jnp.einsum('bhjqd,bhrjkd->bhjqrk', q_r, k_blk).astype(jnp.float32)
        s_loc = jnp.where(valid[:, :, None, :], s_loc + bias.astype(jnp.float32)[None], NEG_INF)
        s_ctx = jnp.einsum('bhjqd,bhkd->bhjqk', q_r, kct).astype(jnp.float32)
        s = jnp.concatenate([s_loc.reshape(b, N_HEADS, NA_NCB, NA_QB_W, n_loc), s_ctx], axis=-1)
        p = jax.nn.softmax(s, axis=-1).astype(v.dtype)
        p_loc = p[..., :n_loc].reshape(b, N_HEADS, NA_NCB, NA_QB_W, kh, NA_CW)
        p_ctx = p[..., n_loc:]
        o = (jnp.einsum('bhjqrk,bhrjkd->bhjqd', p_loc, v_blk)
             + jnp.einsum('bhjqk,bhkd->bhjqd', p_ctx, vct))
        return o.reshape(b, N_HEADS, GRID_W, HEAD_DIM)

    out = lax.map(row, jnp.arange(rows))
    return out.transpose(1, 0, 3, 2, 4).reshape(b, l, D_ATTN)


def _context_attention(qc, kc, vc):
    b, lc = qc.shape[:2]
    s = jnp.einsum('bqhd,bkhd->bhqk', qc * HEAD_DIM ** -0.5, kc).astype(jnp.float32)
    p = jax.nn.softmax(s, axis=-1).astype(vc.dtype)
    return jnp.einsum('bhqk,bkhd->bqhd', p, vc).reshape(b, lc, D_ATTN)


def _merge(y_pool, z_pool, y_attn, z_attn, g_pool, g_attn, w_br_pool, w_br_attn, w_out):
    br_pool = (y_pool * jax.nn.silu(z_pool)) @ w_br_pool
    br_attn = (y_attn * jax.nn.silu(z_attn)) @ w_br_attn
    return (jax.nn.sigmoid(g_pool) * br_pool + jax.nn.sigmoid(g_attn) * br_attn) @ w_out


def setup_inputs(seed: int = 0) -> dict:
    key = jax.random.key(seed)
    ks = jax.random.split(key, 16)
    f32 = jnp.float32
    d = D_MODEL
    nrm = lambda k, shape, s: jax.random.normal(k, shape, f32) * s
    return {
        "x": nrm(ks[0], (BATCH, SEQ, d), 1.0),
        "c": nrm(ks[1], (BATCH, d), 1.0),
        "ctx": nrm(ks[2], (BATCH, CTX_LEN, d), 1.0),
        "c_ctx": nrm(ks[3], (d,), 1.0),
        "norm_g": 1.0 + nrm(ks[4], (DEPTH, d), 0.05),
        "w_ada": nrm(ks[5], (DEPTH, d, 3 * d), 0.5 * d ** -0.5),
        "b_ada": nrm(ks[6], (DEPTH, 3 * d), 0.02),
        "w_in": nrm(ks[7], (DEPTH, d, D_IN), d ** -0.5),
        "b_in": nrm(ks[8], (DEPTH, D_IN), 0.02),
        "w_pool": nrm(ks[9], (DEPTH, N_POOL_GROUPS, D_POOL_GROUP, D_POOL_GROUP), D_POOL_GROUP ** -0.5),
        "s_pool": 1.0 + nrm(ks[10], (DEPTH, D_POOL), 0.1),
        "rpb": nrm(ks[11], (DEPTH, N_HEADS, 2 * NA_KH_MAX - 1, 2 * NA_KW - 1), 0.1),
        "w_br_pool": nrm(ks[12], (DEPTH, D_POOL, d), D_POOL ** -0.5),
        "w_br_attn": nrm(ks[13], (DEPTH, D_ATTN, d), D_ATTN ** -0.5),
        "w_out": nrm(ks[14], (DEPTH, d, d), d ** -0.5),
        "final_g": 1.0 + nrm(ks[15], (d,), 0.05),
    }


def reference(x, c, ctx, c_ctx, norm_g, w_ada, b_ada, w_in, b_in, w_pool, s_pool, rpb,
              w_br_pool, w_br_attn, w_out, final_g):
    x_lat, x_ctx = x, ctx
    kv_lo = 2 * D_POOL + D_ATTN
    for l in range(DEPTH):
        last = l == DEPTH - 1
        ada_lat = jax.nn.silu(c) @ w_ada[l] + b_ada[l]
        sh, sc, gt = jnp.split(ada_lat[:, None, :], 3, axis=-1)
        ada_ctx = jax.nn.silu(c_ctx) @ w_ada[l] + b_ada[l]
        sh_c, sc_c, gt_c = jnp.split(ada_ctx, 3)
        h_lat = _modulate(x_lat, norm_g[l], sh, sc)
        h_ctx = _modulate(x_ctx, norm_g[l], sh_c, sc_c)

        u_l, zp_l, q_l, k_l, v_l, za_l, gp_l, ga_l = _split_proj(h_lat @ w_in[l] + b_in[l])
        if last:
            kv_c = h_ctx @ w_in[l][:, kv_lo:kv_lo + 2 * D_ATTN] + b_in[l][kv_lo:kv_lo + 2 * D_ATTN]
            k_c, v_c = jnp.split(kv_c, 2, axis=-1)
        else:
            u_c, zp_c, q_c, k_c, v_c, za_c, gp_c, ga_c = _split_proj(h_ctx @ w_in[l] + b_in[l])
        kc, vc = _heads(k_c), _heads(v_c)

        y_pool_l = _pool_mix(u_l, w_pool[l], s_pool[l])
        y_attn_l = _neighbourhood_attention(_heads(q_l), _heads(k_l), _heads(v_l), kc, vc, rpb[l])
        mix_lat = _merge(y_pool_l, zp_l, y_attn_l, za_l, gp_l, ga_l, w_br_pool[l], w_br_attn[l], w_out[l])

        if not last:
            y_pool_c = _pool_mix(u_c, w_pool[l], s_pool[l])
            y_attn_c = _context_attention(_heads(q_c), kc, vc)
            mix_ctx = _merge(y_pool_c, zp_c, y_attn_c, za_c, gp_c, ga_c, w_br_pool[l], w_br_attn[l], w_out[l])
            x_ctx = x_ctx + gt_c * mix_ctx
        x_lat = x_lat + gt * mix_lat
    return _rmsnorm(x_lat, final_g)
```

```python
import functools

import numpy as np
import jax
import jax.numpy as jnp
from jax import lax
from jax.experimental import pallas as pl
from jax.experimental.pallas import tpu as pltpu

F32 = jnp.float32
BF16 = jnp.bfloat16

D_MODEL = 4096
GRID_W = 64
LOG2_GRID_W = 6
D_POOL = D_MODEL // 2
POOL_WINDOWS = (2, 4, 8, 16)
D_POOL_GROUP = D_POOL // len(POOL_WINDOWS)
HEAD_DIM = 128
D_ATTN = D_MODEL // 2
N_HEADS = D_ATTN // HEAD_DIM
NA_KH = 8
NA_KW = 16
D_IN = 2 * D_POOL + 4 * D_ATTN + 2 * D_MODEL
RMS_EPS = 1e-6
NEG_INF = -1e30
ATTN_SCALE = HEAD_DIM ** -0.5

COL_U = 0
COL_ZP = D_POOL
COL_Q = 2 * D_POOL
COL_K = COL_Q + D_ATTN
COL_V = COL_K + D_ATTN
COL_ZA = COL_V + D_ATTN
COL_GP = COL_ZA + D_ATTN
COL_GA = COL_GP + D_MODEL

V7X_VMEM_LIMIT_BYTES = 60000 * 1024
BF16_SUBLANE_TILE = 16

QROWS = 4
BAND_ROWS = 12
TQ = QROWS * GRID_W
TK = BAND_ROWS * GRID_W
POOL_HALO = 64


def _params(vmem_bytes, n_axes):
    return pltpu.CompilerParams(
        dimension_semantics=("arbitrary",) * n_axes,
        vmem_limit_bytes=int(min(V7X_VMEM_LIMIT_BYTES, vmem_bytes)),
    )


def _sigmoid(x):
    return 1.0 / (1.0 + jnp.exp(-x))


def _ada_kernel(c_ref, w_ref, b_ref, o_ref):
    cv = c_ref[...]
    s = cv * _sigmoid(cv)
    o_ref[...] = jnp.dot(s, w_ref[...], preferred_element_type=F32,
                         precision=lax.Precision.HIGHEST) + b_ref[...]


def _ada_call(cvec, w_ada, b_ada):
    depth, d, n = w_ada.shape
    tn = 512
    blk = 2 * (d * tn * 4) + 2 * 8 * d * 4 + 4 * 8 * tn * 4
    return pl.pallas_call(
        _ada_kernel,
        grid=(depth, n // tn),
        in_specs=[
            pl.BlockSpec((8, d), lambda l, j: (0, 0)),
            pl.BlockSpec((None, d, tn), lambda l, j: (l, 0, j)),
            pl.BlockSpec((None, 1, tn), lambda l, j: (l, 0, j)),
        ],
        out_specs=pl.BlockSpec((None, 8, tn), lambda l, j: (l, 0, j)),
        out_shape=jax.ShapeDtypeStruct((depth, 8, n), F32),
        compiler_params=_params(blk + (8 << 20), 2),
        name="ada",
    )(cvec, w_ada, b_ada.reshape(depth, 1, n))


def _mod_kernel(x_ref, g_ref, sh_ref, sc_ref, o_ref):
    x = x_ref[...]
    ms = jnp.mean(x * x, axis=-1, keepdims=True)
    y = x * lax.rsqrt(ms + RMS_EPS) * g_ref[...]
    o_ref[...] = (y * (1.0 + sc_ref[0]) + sh_ref[0]).astype(o_ref.dtype)


def _mod_call(x2, g, sh, sc, row_of_tile, tr):
    m, d = x2.shape
    vec = pl.BlockSpec((1, 1, d), lambda i: (row_of_tile(i), 0, 0))
    return pl.pallas_call(
        _mod_kernel,
        grid=(m // tr,),
        in_specs=[
            pl.BlockSpec((tr, d), lambda i: (i, 0)),
            pl.BlockSpec((1, d), lambda i: (0, 0)),
            vec, vec,
        ],
        out_specs=pl.BlockSpec((tr, d), lambda i: (i, 0)),
        out_shape=jax.ShapeDtypeStruct((m, d), BF16),
        compiler_params=_params(2 * tr * d * 6 + 3 * tr * d * 4 + (4 << 20), 1),
        name="modulate",
    )(x2, g.reshape(1, d), sh, sc)


def _rms_kernel(x_ref, g_ref, o_ref):
    x = x_ref[...]
    ms = jnp.mean(x * x, axis=-1, keepdims=True)
    o_ref[...] = x * lax.rsqrt(ms + RMS_EPS) * g_ref[...]


def _rms_call(x2, g, tr):
    m, d = x2.shape
    return pl.pallas_call(
        _rms_kernel,
        grid=(m // tr,),
        in_specs=[pl.BlockSpec((tr, d), lambda i: (i, 0)),
                  pl.BlockSpec((1, d), lambda i: (0, 0))],
        out_specs=pl.BlockSpec((tr, d), lambda i: (i, 0)),
        out_shape=jax.ShapeDtypeStruct((m, d), F32),
        compiler_params=_params(2 * tr * d * 8 + 2 * tr * d * 4 + (4 << 20), 1),
        name="final_norm",
    )(x2, g.reshape(1, d))


def _proj_kernel(a_ref, w_ref, b_ref, o_ref):
    acc = jnp.dot(a_ref[...], w_ref[...], preferred_element_type=F32)
    o_ref[...] = (acc + b_ref[...]).astype(o_ref.dtype)


def _proj_call(a, w, b, tm, tn):
    m, k = a.shape
    n = w.shape[1]
    blk = 2 * (tm * k * 2 + k * tn * 2 + tm * tn * 2 + tn * 4) + tm * tn * 4
    return pl.pallas_call(
        _proj_kernel,
        grid=(m // tm, n // tn),
        in_specs=[
            pl.BlockSpec((tm, k), lambda i, j: (i, 0)),
            pl.BlockSpec((k, tn), lambda i, j: (0, j)),
            pl.BlockSpec((1, tn), lambda i, j: (0, j)),
        ],
        out_specs=pl.BlockSpec((tm, tn), lambda i, j: (i, j)),
        out_shape=jax.ShapeDtypeStruct((m, n), BF16),
        compiler_params=_params(blk + (4 << 20), 2),
        name="in_proj",
    )(a, w, b.reshape(1, n))


def _dot_nt(a, b):
    return lax.dot_general(a, b, (((1,), (1,)), ((), ())), preferred_element_type=F32)


def _softmax_pv(s_loc, s_ctx, v_loc, v_ctx, out_dtype):
    mx = jnp.maximum(jnp.max(s_loc, axis=-1, keepdims=True), jnp.max(s_ctx, axis=-1, keepdims=True))
    e_loc = jnp.exp(s_loc - mx)
    e_ctx = jnp.exp(s_ctx - mx)
    den = jnp.sum(e_loc, axis=-1, keepdims=True) + jnp.sum(e_ctx, axis=-1, keepdims=True)
    o = (jnp.dot(e_loc.astype(BF16), v_loc, preferred_element_type=F32)
         + jnp.dot(e_ctx.astype(BF16), v_ctx, preferred_element_type=F32))
    return (o / den).astype(out_dtype)


def _na_kernel(q_ref, k_ref, v_ref, kc_ref, vc_ref, c2_ref, o_ref, *, n_groups, n_rows):
    g = pl.program_id(2)
    last = n_groups - 1
    kr0 = jnp.clip(QROWS * g - NA_KH // 2, 0, n_rows - BAND_ROWS)
    start = pl.multiple_of(kr0 * GRID_W, GRID_W)
    interior = jnp.logical_and(g > 0, g < last)
    lo_a = jnp.where(g == last, BAND_ROWS - NA_KH, 0)
    lo_b = jnp.where(interior, 1, 0)
    off = jnp.where(g == 0, NA_KH - 1,
                    jnp.where(g == last, NA_KH - 1 - BAND_ROWS + QROWS, NA_KH - 1 - NA_KH // 2))

    q = q_ref[...]
    kb = k_ref[pl.ds(start, TK), :]
    vb = v_ref[pl.ds(start, TK), :]
    s_loc = _dot_nt(q, kb) * ATTN_SCALE
    s_ctx = _dot_nt(q, kc_ref[...]) * ATTN_SCALE

    bias = jnp.concatenate(
        [jnp.concatenate([c2_ref[2 * m - i + off + QROWS] for m in range(BAND_ROWS // 2)], axis=1)
         for i in range(QROWS)], axis=0)
    qrow = jnp.right_shift(lax.broadcasted_iota(jnp.int32, (TQ, 1), 0), LOG2_GRID_W)
    lo = lo_a + lo_b * qrow
    jrow = jnp.right_shift(lax.broadcasted_iota(jnp.int32, (1, TK), 1), LOG2_GRID_W)
    row_ok = jnp.logical_and(jrow >= lo, jrow < lo + NA_KH)
    s_loc = jnp.where(row_ok, s_loc + bias, NEG_INF)
    o_ref[...] = _softmax_pv(s_loc, s_ctx, vb, vc_ref[...], o_ref.dtype)


def _na_call(p, pc, c2, batch, seq, ctx_len, kc_col, vc_col):
    n_rows = seq // GRID_W
    n_groups = n_rows // QROWS
    hb = HEAD_DIM
    n_c2 = c2.shape[1]
    blk = 2 * (TQ * hb * 2 * 2 + 2 * seq * hb * 2 + 2 * ctx_len * hb * 2 + n_c2 * GRID_W * 2 * GRID_W * 4)
    tmp = 6 * TQ * (TK + ctx_len) * 4
    kern = functools.partial(_na_kernel, n_groups=n_groups, n_rows=n_rows)
    return pl.pallas_call(
        kern,
        grid=(batch, N_HEADS, n_groups),
        in_specs=[
            pl.BlockSpec((TQ, hb), lambda b, h, g: (b * n_groups + g, COL_Q // hb + h)),
            pl.BlockSpec((seq, hb), lambda b, h, g: (b, COL_K // hb + h)),
            pl.BlockSpec((seq, hb), lambda b, h, g: (b, COL_V // hb + h)),
            pl.BlockSpec((ctx_len, hb), lambda b, h, g: (b, kc_col // hb + h)),
            pl.BlockSpec((ctx_len, hb), lambda b, h, g: (b, vc_col // hb + h)),
            pl.BlockSpec((None, n_c2, GRID_W, 2 * GRID_W), lambda b, h, g: (h, 0, 0, 0)),
        ],
        out_specs=pl.BlockSpec((TQ, hb), lambda b, h, g: (b * n_groups + g, h)),
        out_shape=jax.ShapeDtypeStruct((batch * seq, D_ATTN), BF16),
        compiler_params=_params(blk + tmp + (4 << 20), 3),
        name="na_attn",
    )(p, p, p, pc, pc, c2)


def _ctx_attn_kernel(q_ref, k_ref, v_ref, o_ref):
    s = _dot_nt(q_ref[...], k_ref[...]) * ATTN_SCALE
    mx = jnp.max(s, axis=-1, keepdims=True)
    e = jnp.exp(s - mx)
    den = jnp.sum(e, axis=-1, keepdims=True)
    o = jnp.dot(e.astype(BF16), v_ref[...], preferred_element_type=F32)
    o_ref[...] = (o / den).astype(o_ref.dtype)


def _ctx_attn_call(pc, batch, ctx_len):
    hb = HEAD_DIM
    return pl.pallas_call(
        _ctx_attn_kernel,
        grid=(batch, N_HEADS),
        in_specs=[
            pl.BlockSpec((ctx_len, hb), lambda b, h: (b, COL_Q // hb + h)),
            pl.BlockSpec((ctx_len, hb), lambda b, h: (b, COL_K // hb + h)),
            pl.BlockSpec((ctx_len, hb), lambda b, h: (b, COL_V // hb + h)),
        ],
        out_specs=pl.BlockSpec((ctx_len, hb), lambda b, h: (b, h)),
        out_shape=jax.ShapeDtypeStruct((batch * ctx_len, D_ATTN), BF16),
        compiler_params=_params(8 * ctx_len * hb * 2 + 8 * ctx_len * ctx_len * 4 + (4 << 20), 2),
        name="ctx_attn",
    )(pc, pc, pc)


def _bias_pair_table(rpb_l):
    qc = np.arange(GRID_W)[:, None]
    kc = np.arange(GRID_W)[None, :]
    ws = np.clip(qc - NA_KW // 2, 0, GRID_W - NA_KW)
    col_ok = (kc >= ws) & (kc < ws + NA_KW)
    dc = kc - qc + NA_KW - 1
    onehot = (dc[None] == np.arange(2 * NA_KW - 1)[:, None, None]) & col_ok[None]
    full = jnp.einsum("hdx,xqk->hdqk", rpb_l, jnp.asarray(onehot, F32), precision=lax.Precision.HIGHEST)
    full = full + jnp.asarray(np.where(col_ok, 0.0, NEG_INF), F32)
    n_e = BAND_ROWS + NA_KH - 1 + QROWS - 1
    n_dr = 2 * NA_KH - 1
    left = np.clip(np.arange(n_e) - QROWS, 0, n_dr - 1)
    right = np.clip(np.arange(n_e) - QROWS + 1, 0, n_dr - 1)
    return jnp.concatenate([full[:, left], full[:, right]], axis=-1)


def _pool_gate_kernel(ucur_ref, uprev_ref, unext_ref, zp_ref, za_ref, ya_ref, wp_ref, sp_ref, o_ref,
                      *, tiles_per_seq, seq):
    t = ucur_ref.shape[0]
    cg = D_POOL_GROUP
    base = (pl.program_id(0) % tiles_per_seq) * t
    ucat = jnp.concatenate([uprev_ref[...], ucur_ref[...], unext_ref[...]], axis=0)
    tc = t + 2 * POOL_HALO
    row = lax.broadcasted_iota(jnp.int32, (t, tc), 0) + base
    col = lax.broadcasted_iota(jnp.int32, (t, tc), 1) + (base - POOL_HALO)
    tpos = lax.broadcasted_iota(jnp.int32, (t, 1), 0) + base
    for gi, w in enumerate(POOL_WINDOWS):
        cols = slice(gi * cg, (gi + 1) * cg)
        lo = jnp.maximum(row - w // 2, 0)
        hi = jnp.minimum(row - w // 2 + w - 1, seq - 1)
        band = jnp.where(jnp.logical_and(col >= lo, col <= hi), 1.0, 0.0).astype(BF16)
        wsum = jnp.dot(band, ucat[:, cols], preferred_element_type=F32)
        cnt = (jnp.minimum(tpos - w // 2 + w - 1, seq - 1) - jnp.maximum(tpos - w // 2, 0) + 1).astype(F32)
        pooled = wsum / cnt - ucur_ref[:, cols].astype(F32)
        y = jnp.dot(pooled.astype(BF16), wp_ref[gi], preferred_element_type=F32) * sp_ref[:, cols]
        z = zp_ref[:, cols].astype(F32)
        o_ref[:, cols] = (y * (z * _sigmoid(z))).astype(o_ref.dtype)
    za = za_ref[...].astype(F32)
    o_ref[:, D_POOL:] = (ya_ref[...].astype(F32) * (za * _sigmoid(za))).astype(o_ref.dtype)


def _pool_gate_call(p, y_attn, w_pool_b, s_pool_l, seq, t):
    m = p.shape[0]
    tiles_per_seq = seq // t
    hpt = t // POOL_HALO
    n_halo_blocks = m // POOL_HALO
    blk = 2 * (3 * t * D_POOL * 2 + 2 * POOL_HALO * D_POOL * 2 + t * D_ATTN * 2 + t * D_MODEL * 2
               + w_pool_b.size * 2 + D_POOL * 4)
    tmp = 4 * t * (t + 2 * POOL_HALO) * 4 + 8 * t * D_POOL_GROUP * 4
    kern = functools.partial(_pool_gate_kernel, tiles_per_seq=tiles_per_seq, seq=seq)
    return pl.pallas_call(
        kern,
        grid=(m // t,),
        in_specs=[
            pl.BlockSpec((t, D_POOL), lambda i: (i, COL_U // D_POOL)),
            pl.BlockSpec((POOL_HALO, D_POOL), lambda i: (jnp.maximum(i * hpt - 1, 0), COL_U // D_POOL)),
            pl.BlockSpec((POOL_HALO, D_POOL),
                         lambda i: (jnp.minimum((i + 1) * hpt, n_halo_blocks - 1), COL_U // D_POOL)),
            pl.BlockSpec((t, D_POOL), lambda i: (i, COL_ZP // D_POOL)),
            pl.BlockSpec((t, D_ATTN), lambda i: (i, COL_ZA // D_ATTN)),
            pl.BlockSpec((t, D_ATTN), lambda i: (i, 0)),
            pl.BlockSpec(w_pool_b.shape, lambda i: (0, 0, 0)),
            pl.BlockSpec((1, D_POOL), lambda i: (0, 0)),
        ],
        out_specs=pl.BlockSpec((t, D_MODEL), lambda i: (i, 0)),
        out_shape=jax.ShapeDtypeStruct((m, D_MODEL), BF16),
        compiler_params=_params(blk + tmp + (4 << 20), 1),
        name="pool_gate",
    )(p, p, p, p, p, y_attn, w_pool_b, s_pool_l.reshape(1, D_POOL))


def _merge_kernel(a_ref, wp_ref, wa_ref, gp_ref, ga_ref, o_ref):
    br_p = jnp.dot(a_ref[:, :D_POOL], wp_ref[...], preferred_element_type=F32)
    br_a = jnp.dot(a_ref[:, D_POOL:], wa_ref[...], preferred_element_type=F32)
    gp = _sigmoid(gp_ref[...].astype(F32))
    ga = _sigmoid(ga_ref[...].astype(F32))
    o_ref[...] = (gp * br_p + ga * br_a).astype(o_ref.dtype)


def _merge_call(a, w_br_pool_b, w_br_attn_b, p, tm, tn):
    m = a.shape[0]
    n = D_MODEL
    blk = 2 * (tm * D_MODEL * 2 + 2 * D_POOL * tn * 2 + 3 * tm * tn * 2)
    tmp = 4 * tm * tn * 4
    return pl.pallas_call(
        _merge_kernel,
        grid=(m // tm, n // tn),
        in_specs=[
            pl.BlockSpec((tm, D_MODEL), lambda i, j: (i, 0)),
            pl.BlockSpec((D_POOL, tn), lambda i, j: (0, j)),
            pl.BlockSpec((D_ATTN, tn), lambda i, j: (0, j)),
            pl.BlockSpec((tm, tn), lambda i, j: (i, COL_GP // tn + j)),
            pl.BlockSpec((tm, tn), lambda i, j: (i, COL_GA // tn + j)),
        ],
        out_specs=pl.BlockSpec((tm, tn), lambda i, j: (i, j)),
        out_shape=jax.ShapeDtypeStruct((m, n), BF16),
        compiler_params=_params(blk + tmp + (4 << 20), 2),
        name="merge",
    )(a, w_br_pool_b, w_br_attn_b, p, p)


def _out_kernel(m_ref, w_ref, x_ref, gt_ref, o_ref):
    acc = jnp.dot(m_ref[...], w_ref[...], preferred_element_type=F32)
    o_ref[...] = x_ref[...] + gt_ref[0] * acc


def _out_call(mix, w_out_b, x2, gt, row_of_tile, tm, tn):
    m, k = mix.shape
    n = w_out_b.shape[1]
    blk = 2 * (tm * k * 2 + k * tn * 2 + 2 * tm * tn * 4 + tn * 4)
    tmp = 2 * tm * tn * 4
    return pl.pallas_call(
        _out_kernel,
        grid=(m // tm, n // tn),
        in_specs=[
            pl.BlockSpec((tm, k), lambda i, j: (i, 0)),
            pl.BlockSpec((k, tn), lambda i, j: (0, j)),
            pl.BlockSpec((tm, tn), lambda i, j: (i, j)),
            pl.BlockSpec((1, 1, tn), lambda i, j: (row_of_tile(i), 0, j)),
        ],
        out_specs=pl.BlockSpec((tm, tn), lambda i, j: (i, j)),
        out_shape=jax.ShapeDtypeStruct((m, n), F32),
        compiler_params=_params(blk + tmp + (4 << 20), 2),
        name="out_proj",
    )(mix, w_out_b, x2, gt)


def kernel(x, c, ctx, c_ctx, norm_g, w_ada, b_ada, w_in, b_in, w_pool, s_pool, rpb,
           w_br_pool, w_br_attn, w_out, final_g):
    batch, seq, d = x.shape
    ctx_len = ctx.shape[1]
    depth = w_in.shape[0]
    ctx_row = batch

    x_lat = x.reshape(batch * seq, d)
    x_ctx = ctx.reshape(batch * ctx_len, d)
    cvec = jnp.zeros((8, d), F32).at[:batch].set(c).at[ctx_row].set(c_ctx)
    ada = _ada_call(cvec, w_ada, b_ada)

    tm_lat, tm_merge, tn = 1024, 512, 1024
    tm_ctx = batch * ctx_len
    t_row = 256
    lat_row = lambda tile_rows: (lambda i: i // (seq // tile_rows))
    ctx_row_fn = lambda i: ctx_row

    for l in range(depth):
        last = l == depth - 1
        mod = ada[l].reshape(8, 3, 1, d)
        sh, sc, gt = mod[:, 0], mod[:, 1], mod[:, 2]
        w_in_b = w_in[l].astype(BF16)
        w_pool_b = w_pool[l].astype(BF16)
        w_brp_b = w_br_pool[l].astype(BF16)
        w_bra_b = w_br_attn[l].astype(BF16)
        w_out_b = w_out[l].astype(BF16)
        c2 = _bias_pair_table(rpb[l])

        h_lat = _mod_call(x_lat, norm_g[l], sh, sc, lat_row(t_row), t_row)
        h_ctx = _mod_call(x_ctx, norm_g[l], sh, sc, ctx_row_fn, t_row)
        p_lat = _proj_call(h_lat, w_in_b, b_in[l], tm_lat, tn)
        if last:
            p_ctx = _proj_call(h_ctx, w_in_b[:, COL_K:COL_K + 2 * D_ATTN],
                               b_in[l][COL_K:COL_K + 2 * D_ATTN], tm_ctx, tn)
            kc_col, vc_col = 0, D_ATTN
        else:
            p_ctx = _proj_call(h_ctx, w_in_b, b_in[l], tm_ctx, tn)
            kc_col, vc_col = COL_K, COL_V

        y_attn = _na_call(p_lat, p_ctx, c2, batch, seq, ctx_len, kc_col, vc_col)
        a_lat = _pool_gate_call(p_lat, y_attn, w_pool_b, s_pool[l], seq, t_row)
        mix = _merge_call(a_lat, w_brp_b, w_bra_b, p_lat, tm_merge, tn)
        x_lat_new = _out_call(mix, w_out_b, x_lat, gt, lat_row(tm_lat), tm_lat, tn)

        if not last:
            y_attn_c = _ctx_attn_call(p_ctx, batch, ctx_len)
            a_ctx = _pool_gate_call(p_ctx, y_attn_c, w_pool_b, s_pool[l], ctx_len, t_row)
            mix_c = _merge_call(a_ctx, w_brp_b, w_bra_b, p_ctx, tm_ctx, tn)
            x_ctx = _out_call(mix_c, w_out_b, x_ctx, gt, ctx_row_fn, tm_ctx, tn)
        x_lat = x_lat_new

    return _rms_call(x_lat, final_g, t_row).reshape(batch, seq, d)
```

```python
import functools

import numpy as np
import jax
import jax.numpy as jnp
from jax import lax
from jax.experimental import pallas as pl
from jax.experimental.pallas import tpu as pltpu

F32 = jnp.float32
BF16 = jnp.bfloat16

D_MODEL = 4096
GRID_W = 64
LOG2_GRID_W = 6
D_POOL = D_MODEL // 2
POOL_WINDOWS = (2, 4, 8, 16)
D_POOL_GROUP = D_POOL // len(POOL_WINDOWS)
HEAD_DIM = 128
D_ATTN = D_MODEL // 2
N_HEADS = D_ATTN // HEAD_DIM
NA_KH = 8
NA_KW = 16
D_IN = 2 * D_POOL + 4 * D_ATTN + 2 * D_MODEL
RMS_EPS = 1e-6
NEG_INF = -1e30
ATTN_SCALE = HEAD_DIM ** -0.5

COL_U = 0
COL_ZP = D_POOL
COL_Q = 2 * D_POOL
COL_K = COL_Q + D_ATTN
COL_V = COL_K + D_ATTN
COL_ZA = COL_V + D_ATTN
COL_GP = COL_ZA + D_ATTN
COL_GA = COL_GP + D_MODEL

V7X_VMEM_LIMIT_BYTES = 60000 * 1024

QROWS = 4
BAND_ROWS = 12
TQ = QROWS * GRID_W
TK = BAND_ROWS * GRID_W
HEADS_PER_STEP = 4
POOL_HALO = 64


def _params(vmem_bytes, n_axes):
    return pltpu.CompilerParams(
        dimension_semantics=("arbitrary",) * n_axes,
        vmem_limit_bytes=int(min(V7X_VMEM_LIMIT_BYTES, vmem_bytes)),
    )


def _sigmoid(x):
    return 1.0 / (1.0 + jnp.exp(-x))


def _ada_kernel(c_ref, w_ref, b_ref, o_ref):
    cv = c_ref[...]
    s = cv * _sigmoid(cv)
    o_ref[...] = jnp.dot(s, w_ref[...], preferred_element_type=F32,
                         precision=lax.Precision.HIGHEST) + b_ref[...]


def _ada_call(cvec, w_ada, b_ada):
    depth, d, n = w_ada.shape
    tn = 512
    blk = 2 * (d * tn * 4) + 2 * 8 * d * 4 + 4 * 8 * tn * 4
    return pl.pallas_call(
        _ada_kernel,
        grid=(depth, n // tn),
        in_specs=[
            pl.BlockSpec((8, d), lambda l, j: (0, 0)),
            pl.BlockSpec((None, d, tn), lambda l, j: (l, 0, j)),
            pl.BlockSpec((None, 1, tn), lambda l, j: (l, 0, j)),
        ],
        out_specs=pl.BlockSpec((None, 8, tn), lambda l, j: (l, 0, j)),
        out_shape=jax.ShapeDtypeStruct((depth, 8, n), F32),
        compiler_params=_params(blk + (8 << 20), 2),
        name="ada",
    )(cvec, w_ada, b_ada.reshape(depth, 1, n))


def _mod_kernel(x_ref, g_ref, sh_ref, sc_ref, o_ref):
    x = x_ref[...]
    ms = jnp.mean(x * x, axis=-1, keepdims=True)
    y = x * lax.rsqrt(ms + RMS_EPS) * g_ref[...]
    o_ref[...] = (y * (1.0 + sc_ref[0]) + sh_ref[0]).astype(o_ref.dtype)


def _mod_call(x2, g, sh, sc, row_of_tile, tr):
    m, d = x2.shape
    vec = pl.BlockSpec((1, 1, d), lambda i: (row_of_tile(i), 0, 0))
    return pl.pallas_call(
        _mod_kernel,
        grid=(m // tr,),
        in_specs=[
            pl.BlockSpec((tr, d), lambda i: (i, 0)),
            pl.BlockSpec((1, d), lambda i: (0, 0)),
            vec, vec,
        ],
        out_specs=pl.BlockSpec((tr, d), lambda i: (i, 0)),
        out_shape=jax.ShapeDtypeStruct((m, d), BF16),
        compiler_params=_params(2 * tr * d * 6 + 3 * tr * d * 4 + (4 << 20), 1),
        name="modulate",
    )(x2, g.reshape(1, d), sh, sc)


def _rms_kernel(x_ref, g_ref, o_ref):
    x = x_ref[...]
    ms = jnp.mean(x * x, axis=-1, keepdims=True)
    o_ref[...] = x * lax.rsqrt(ms + RMS_EPS) * g_ref[...]


def _rms_call(x2, g, tr):
    m, d = x2.shape
    return pl.pallas_call(
        _rms_kernel,
        grid=(m // tr,),
        in_specs=[pl.BlockSpec((tr, d), lambda i: (i, 0)),
                  pl.BlockSpec((1, d), lambda i: (0, 0))],
        out_specs=pl.BlockSpec((tr, d), lambda i: (i, 0)),
        out_shape=jax.ShapeDtypeStruct((m, d), F32),
        compiler_params=_params(2 * tr * d * 8 + 2 * tr * d * 4 + (4 << 20), 1),
        name="final_norm",
    )(x2, g.reshape(1, d))


def _proj_kernel(a_ref, w_ref, b_ref, o_ref):
    acc = jnp.dot(a_ref[...], w_ref[...].astype(BF16), preferred_element_type=F32)
    o_ref[...] = (acc + b_ref[...]).astype(o_ref.dtype)


def _proj_call(a, w_stack, b_stack, layer, col0, n, tm, tn):
    m, k = a.shape
    depth, _, n_all = w_stack.shape
    jb = col0 // tn
    blk = 2 * (tm * k * 2 + k * tn * 4 + tm * tn * 2 + tn * 4)
    tmp = k * tn * 2 + tm * tn * 4
    return pl.pallas_call(
        _proj_kernel,
        grid=(m // tm, n // tn),
        in_specs=[
            pl.BlockSpec((tm, k), lambda i, j: (i, 0)),
            pl.BlockSpec((None, k, tn), lambda i, j: (layer, 0, jb + j)),
            pl.BlockSpec((None, 1, tn), lambda i, j: (layer, 0, jb + j)),
        ],
        out_specs=pl.BlockSpec((tm, tn), lambda i, j: (i, j)),
        out_shape=jax.ShapeDtypeStruct((m, n), BF16),
        compiler_params=_params(blk + tmp + (4 << 20), 2),
        name="in_proj",
    )(a, w_stack, b_stack.reshape(depth, 1, n_all))


def _dot_nt(a, b):
    return lax.dot_general(a, b, (((1,), (1,)), ((), ())), preferred_element_type=F32)


def _scaled_q(q_bf16):
    return (q_bf16.astype(F32) * ATTN_SCALE).astype(BF16)


def _softmax_pv(s_loc, s_ctx, v_loc, v_ctx, out_dtype):
    mx = jnp.maximum(jnp.max(s_loc, axis=-1, keepdims=True), jnp.max(s_ctx, axis=-1, keepdims=True))
    e_loc = jnp.exp(s_loc - mx)
    e_ctx = jnp.exp(s_ctx - mx)
    den = jnp.sum(e_loc, axis=-1, keepdims=True) + jnp.sum(e_ctx, axis=-1, keepdims=True)
    o = (jnp.dot(e_loc.astype(BF16), v_loc, preferred_element_type=F32)
         + jnp.dot(e_ctx.astype(BF16), v_ctx, preferred_element_type=F32))
    return (o / den).astype(out_dtype)


def _na_kernel(q_ref, k_ref, v_ref, kc_ref, vc_ref, c2_ref, o_ref, tab_ref, *, n_groups, n_rows):
    g = pl.program_id(2)
    last = n_groups - 1
    kr0 = jnp.clip(QROWS * g - NA_KH // 2, 0, n_rows - BAND_ROWS)
    start = pl.multiple_of(kr0 * GRID_W, GRID_W)

    @pl.when(jnp.logical_or(g <= 1, g == last))
    def _():
        interior = jnp.logical_and(g > 0, g < last)
        lo_a = jnp.where(g == last, BAND_ROWS - NA_KH, 0)
        lo_b = jnp.where(interior, 1, 0)
        off = jnp.where(g == 0, NA_KH - 1,
                        jnp.where(g == last, NA_KH - 1 - BAND_ROWS + QROWS, NA_KH - 1 - NA_KH // 2))
        qrow = jnp.right_shift(lax.broadcasted_iota(jnp.int32, (TQ, 1), 0), LOG2_GRID_W)
        lo = lo_a + lo_b * qrow
        jrow = jnp.right_shift(lax.broadcasted_iota(jnp.int32, (1, TK), 1), LOG2_GRID_W)
        row_ok = jnp.logical_and(jrow >= lo, jrow < lo + NA_KH)
        for hh in range(HEADS_PER_STEP):
            bias = jnp.concatenate(
                [jnp.concatenate([c2_ref[hh, 2 * m - i + off + QROWS] for m in range(BAND_ROWS // 2)], axis=1)
                 for i in range(QROWS)], axis=0)
            tab_ref[hh] = jnp.where(row_ok, bias, NEG_INF)

    for hh in range(HEADS_PER_STEP):
        lanes = slice(hh * HEAD_DIM, (hh + 1) * HEAD_DIM)
        q = _scaled_q(q_ref[:, lanes])
        kb = k_ref[pl.ds(start, TK), lanes]
        vb = v_ref[pl.ds(start, TK), lanes]
        s_loc = _dot_nt(q, kb) + tab_ref[hh]
        s_ctx = _dot_nt(q, kc_ref[:, lanes])
        o_ref[:, lanes] = _softmax_pv(s_loc, s_ctx, vb, vc_ref[:, lanes], o_ref.dtype)


def _na_call(p, pc, c2, batch, seq, ctx_len, kc_col, vc_col):
    n_rows = seq // GRID_W
    n_groups = n_rows // QROWS
    wb = HEADS_PER_STEP * HEAD_DIM
    n_c2 = c2.shape[1]
    c2_bytes = HEADS_PER_STEP * n_c2 * GRID_W * 2 * GRID_W * 4
    tab_bytes = HEADS_PER_STEP * TQ * TK * 4
    blk = 2 * (TQ * wb * 2 * 2 + 2 * seq * wb * 2 + 2 * ctx_len * wb * 2 + c2_bytes)
    tmp = HEADS_PER_STEP * 4 * TQ * (TK + ctx_len) * 4
    kern = functools.partial(_na_kernel, n_groups=n_groups, n_rows=n_rows)
    return pl.pallas_call(
        kern,
        grid=(batch, N_HEADS // HEADS_PER_STEP, n_groups),
        in_specs=[
            pl.BlockSpec((TQ, wb), lambda b, h, g: (b * n_groups + g, COL_Q // wb + h)),
            pl.BlockSpec((seq, wb), lambda b, h, g: (b, COL_K // wb + h)),
            pl.BlockSpec((seq, wb), lambda b, h, g: (b, COL_V // wb + h)),
            pl.BlockSpec((ctx_len, wb), lambda b, h, g: (b, kc_col // wb + h)),
            pl.BlockSpec((ctx_len, wb), lambda b, h, g: (b, vc_col // wb + h)),
            pl.BlockSpec((HEADS_PER_STEP, n_c2, GRID_W, 2 * GRID_W), lambda b, h, g: (h, 0, 0, 0)),
        ],
        out_specs=pl.BlockSpec((TQ, wb), lambda b, h, g: (b * n_groups + g, h)),
        out_shape=jax.ShapeDtypeStruct((batch * seq, D_ATTN), BF16),
        scratch_shapes=[pltpu.VMEM((HEADS_PER_STEP, TQ, TK), F32)],
        compiler_params=_params(blk + tab_bytes + tmp + (4 << 20), 3),
        name="na_attn",
    )(p, p, p, pc, pc, c2)


def _ctx_attn_kernel(q_ref, k_ref, v_ref, o_ref):
    s = _dot_nt(_scaled_q(q_ref[...]), k_ref[...])
    mx = jnp.max(s, axis=-1, keepdims=True)
    e = jnp.exp(s - mx)
    den = jnp.sum(e, axis=-1, keepdims=True)
    o = jnp.dot(e.astype(BF16), v_ref[...], preferred_element_type=F32)
    o_ref[...] = (o / den).astype(o_ref.dtype)


def _ctx_attn_call(pc, batch, ctx_len):
    hb = HEAD_DIM
    return pl.pallas_call(
        _ctx_attn_kernel,
        grid=(batch, N_HEADS),
        in_specs=[
            pl.BlockSpec((ctx_len, hb), lambda b, h: (b, COL_Q // hb + h)),
            pl.BlockSpec((ctx_len, hb), lambda b, h: (b, COL_K // hb + h)),
            pl.BlockSpec((ctx_len, hb), lambda b, h: (b, COL_V // hb + h)),
        ],
        out_specs=pl.BlockSpec((ctx_len, hb), lambda b, h: (b, h)),
        out_shape=jax.ShapeDtypeStruct((batch * ctx_len, D_ATTN), BF16),
        compiler_params=_params(8 * ctx_len * hb * 2 + 8 * ctx_len * ctx_len * 4 + (4 << 20), 2),
        name="ctx_attn",
    )(pc, pc, pc)


def _bias_pair_table(rpb_l):
    qc = np.arange(GRID_W)[:, None]
    kc = np.arange(GRID_W)[None, :]
    ws = np.clip(qc - NA_KW // 2, 0, GRID_W - NA_KW)
    col_ok = (kc >= ws) & (kc < ws + NA_KW)
    dc = kc - qc + NA_KW - 1
    onehot = (dc[None] == np.arange(2 * NA_KW - 1)[:, None, None]) & col_ok[None]
    full = jnp.einsum("hdx,xqk->hdqk", rpb_l, jnp.asarray(onehot, F32), precision=lax.Precision.HIGHEST)
    full = full + jnp.asarray(np.where(col_ok, 0.0, NEG_INF), F32)
    n_e = BAND_ROWS + NA_KH - 1 + QROWS - 1
    n_dr = 2 * NA_KH - 1
    left = np.clip(np.arange(n_e) - QROWS, 0, n_dr - 1)
    right = np.clip(np.arange(n_e) - QROWS + 1, 0, n_dr - 1)
    return jnp.concatenate([full[:, left], full[:, right]], axis=-1)


def _pool_gate_kernel(ucur_ref, uprev_ref, unext_ref, zp_ref, za_ref, ya_ref, wp_ref, sp_ref, o_ref,
                      *, tiles_per_seq, seq):
    t = ucur_ref.shape[0]
    cg = D_POOL_GROUP
    base = (pl.program_id(0) % tiles_per_seq) * t
    ucat = jnp.concatenate([uprev_ref[...], ucur_ref[...], unext_ref[...]], axis=0)
    tc = t + 2 * POOL_HALO
    row = lax.broadcasted_iota(jnp.int32, (t, tc), 0) + base
    col = lax.broadcasted_iota(jnp.int32, (t, tc), 1) + (base - POOL_HALO)
    tpos = lax.broadcasted_iota(jnp.int32, (t, 1), 0) + base
    for gi, w in enumerate(POOL_WINDOWS):
        cols = slice(gi * cg, (gi + 1) * cg)
        lo = jnp.maximum(row - w // 2, 0)
        hi = jnp.minimum(row - w // 2 + w - 1, seq - 1)
        band = jnp.where(jnp.logical_and(col >= lo, col <= hi), 1.0, 0.0).astype(BF16)
        wsum = jnp.dot(band, ucat[:, cols], preferred_element_type=F32)
        cnt = (jnp.minimum(tpos - w // 2 + w - 1, seq - 1) - jnp.maximum(tpos - w // 2, 0) + 1).astype(F32)
        pooled = wsum / cnt - ucur_ref[:, cols].astype(F32)
        y = jnp.dot(pooled.astype(BF16), wp_ref[gi].astype(BF16), preferred_element_type=F32) * sp_ref[:, cols]
        z = zp_ref[:, cols].astype(F32)
        o_ref[:, cols] = (y * (z * _sigmoid(z))).astype(o_ref.dtype)
    za = za_ref[...].astype(F32)
    o_ref[:, D_POOL:] = (ya_ref[...].astype(F32) * (za * _sigmoid(za))).astype(o_ref.dtype)


def _pool_gate_call(p, y_attn, w_pool, s_pool_l, layer, seq, t):
    m = p.shape[0]
    tiles_per_seq = seq // t
    hpt = t // POOL_HALO
    n_halo_blocks = m // POOL_HALO
    wp_shape = w_pool.shape[1:]
    blk = 2 * (3 * t * D_POOL * 2 + 2 * POOL_HALO * D_POOL * 2 + t * D_ATTN * 2 + t * D_MODEL * 2
               + int(np.prod(wp_shape)) * 4 + D_POOL * 4)
    tmp = 4 * t * (t + 2 * POOL_HALO) * 4 + 8 * t * D_POOL_GROUP * 4
    kern = functools.partial(_pool_gate_kernel, tiles_per_seq=tiles_per_seq, seq=seq)
    return pl.pallas_call(
        kern,
        grid=(m // t,),
        in_specs=[
            pl.BlockSpec((t, D_POOL), lambda i: (i, COL_U // D_POOL)),
            pl.BlockSpec((POOL_HALO, D_POOL), lambda i: (jnp.maximum(i * hpt - 1, 0), COL_U // D_POOL)),
            pl.BlockSpec((POOL_HALO, D_POOL),
                         lambda i: (jnp.minimum((i + 1) * hpt, n_halo_blocks - 1), COL_U // D_POOL)),
            pl.BlockSpec((t, D_POOL), lambda i: (i, COL_ZP // D_POOL)),
            pl.BlockSpec((t, D_ATTN), lambda i: (i, COL_ZA // D_ATTN)),
            pl.BlockSpec((t, D_ATTN), lambda i: (i, 0)),
            pl.BlockSpec((None,) + wp_shape, lambda i: (layer, 0, 0, 0)),
            pl.BlockSpec((1, D_POOL), lambda i: (0, 0)),
        ],
        out_specs=pl.BlockSpec((t, D_MODEL), lambda i: (i, 0)),
        out_shape=jax.ShapeDtypeStruct((m, D_MODEL), BF16),
        compiler_params=_params(blk + tmp + (4 << 20), 1),
        name="pool_gate",
    )(p, p, p, p, p, y_attn, w_pool, s_pool_l.reshape(1, D_POOL))


def _merge_kernel(a_ref, wp_ref, wa_ref, gp_ref, ga_ref, o_ref):
    br_p = jnp.dot(a_ref[:, :D_POOL], wp_ref[...].astype(BF16), preferred_element_type=F32)
    br_a = jnp.dot(a_ref[:, D_POOL:], wa_ref[...].astype(BF16), preferred_element_type=F32)
    gp = _sigmoid(gp_ref[...].astype(F32))
    ga = _sigmoid(ga_ref[...].astype(F32))
    o_ref[...] = (gp * br_p + ga * br_a).astype(o_ref.dtype)


def _merge_call(a, w_br_pool, w_br_attn, p, layer, tm, tn):
    m = a.shape[0]
    n = D_MODEL
    blk = 2 * (tm * D_MODEL * 2 + 2 * D_POOL * tn * 4 + 3 * tm * tn * 2)
    tmp = 2 * D_POOL * tn * 2 + 4 * tm * tn * 4
    return pl.pallas_call(
        _merge_kernel,
        grid=(m // tm, n // tn),
        in_specs=[
            pl.BlockSpec((tm, D_MODEL), lambda i, j: (i, 0)),
            pl.BlockSpec((None, D_POOL, tn), lambda i, j: (layer, 0, j)),
            pl.BlockSpec((None, D_ATTN, tn), lambda i, j: (layer, 0, j)),
            pl.BlockSpec((tm, tn), lambda i, j: (i, COL_GP // tn + j)),
            pl.BlockSpec((tm, tn), lambda i, j: (i, COL_GA // tn + j)),
        ],
        out_specs=pl.BlockSpec((tm, tn), lambda i, j: (i, j)),
        out_shape=jax.ShapeDtypeStruct((m, n), BF16),
        compiler_params=_params(blk + tmp + (4 << 20), 2),
        name="merge",
    )(a, w_br_pool, w_br_attn, p, p)


def _out_kernel(m_ref, w_ref, x_ref, gt_ref, o_ref):
    acc = jnp.dot(m_ref[...], w_ref[...].astype(BF16), preferred_element_type=F32)
    o_ref[...] = x_ref[...] + gt_ref[0] * acc


def _out_call(mix, w_out, x2, gt, layer, row_of_tile, tm, tn):
    m, k = mix.shape
    n = w_out.shape[2]
    blk = 2 * (tm * k * 2 + k * tn * 4 + 2 * tm * tn * 4 + tn * 4)
    tmp = k * tn * 2 + 2 * tm * tn * 4
    return pl.pallas_call(
        _out_kernel,
        grid=(m // tm, n // tn),
        in_specs=[
            pl.BlockSpec((tm, k), lambda i, j: (i, 0)),
            pl.BlockSpec((None, k, tn), lambda i, j: (layer, 0, j)),
            pl.BlockSpec((tm, tn), lambda i, j: (i, j)),
            pl.BlockSpec((1, 1, tn), lambda i, j: (row_of_tile(i), 0, j)),
        ],
        out_specs=pl.BlockSpec((tm, tn), lambda i, j: (i, j)),
        out_shape=jax.ShapeDtypeStruct((m, n), F32),
        compiler_params=_params(blk + tmp + (4 << 20), 2),
        name="out_proj",
    )(mix, w_out, x2, gt)


def kernel(x, c, ctx, c_ctx, norm_g, w_ada, b_ada, w_in, b_in, w_pool, s_pool, rpb,
           w_br_pool, w_br_attn, w_out, final_g):
    batch, seq, d = x.shape
    ctx_len = ctx.shape[1]
    depth = w_in.shape[0]
    ctx_row = batch

    x_lat = x.reshape(batch * seq, d)
    x_ctx = ctx.reshape(batch * ctx_len, d)
    cvec = jnp.zeros((8, d), F32).at[:batch].set(c).at[ctx_row].set(c_ctx)
    ada = _ada_call(cvec, w_ada, b_ada)

    tm_lat, tn = 1024, 512
    tm_ctx = batch * ctx_len
    t_row = 256
    lat_row = lambda tile_rows: (lambda i: i // (seq // tile_rows))
    ctx_row_fn = lambda i: ctx_row

    for l in range(depth):
        last = l == depth - 1
        mod = ada[l].reshape(8, 3, 1, d)
        sh, sc, gt = mod[:, 0], mod[:, 1], mod[:, 2]
        c2 = _bias_pair_table(rpb[l])

        h_lat = _mod_call(x_lat, norm_g[l], sh, sc, lat_row(t_row), t_row)
        h_ctx = _mod_call(x_ctx, norm_g[l], sh, sc, ctx_row_fn, t_row)
        p_lat = _proj_call(h_lat, w_in, b_in, l, 0, D_IN, tm_lat, tn)
        if last:
            p_ctx = _proj_call(h_ctx, w_in, b_in, l, COL_K, 2 * D_ATTN, tm_ctx, tn)
            kc_col, vc_col = 0, D_ATTN
        else:
            p_ctx = _proj_call(h_ctx, w_in, b_in, l, 0, D_IN, tm_ctx, tn)
            kc_col, vc_col = COL_K, COL_V

        y_attn = _na_call(p_lat, p_ctx, c2, batch, seq, ctx_len, kc_col, vc_col)
        a_lat = _pool_gate_call(p_lat, y_attn, w_pool, s_pool[l], l, seq, t_row)
        mix = _merge_call(a_lat, w_br_pool, w_br_attn, p_lat, l, tm_lat, tn)
        x_lat_new = _out_call(mix, w_out, x_lat, gt, l, lat_row(tm_lat), tm_lat, tn)

        if not last:
            y_attn_c = _ctx_attn_call(p_ctx, batch, ctx_len)
            a_ctx = _pool_gate_call(p_ctx, y_attn_c, w_pool, s_pool[l], l, ctx_len, t_row)
            mix_c = _merge_call(a_ctx, w_br_pool, w_br_attn, p_ctx, l, tm_ctx, tn)
            x_ctx = _out_call(mix_c, w_out, x_ctx, gt, l, ctx_row_fn, tm_ctx, tn)
        x_lat = x_lat_new

    return _rms_call(x_lat, final_g, t_row).reshape(batch, seq, d)
```

```python
import functools

import numpy as np
import jax
import jax.numpy as jnp
from jax import lax
from jax.experimental import pallas as pl
from jax.experimental.pallas import tpu as pltpu

F32 = jnp.float32
BF16 = jnp.bfloat16

D_MODEL = 4096
GRID_W = 64
LOG2_GRID_W = 6
D_POOL = D_MODEL // 2
POOL_WINDOWS = (2, 4, 8, 16)
D_POOL_GROUP = D_POOL // len(POOL_WINDOWS)
HEAD_DIM = 128
D_ATTN = D_MODEL // 2
N_HEADS = D_ATTN // HEAD_DIM
NA_KH = 8
NA_KW = 16
D_IN = 2 * D_POOL + 4 * D_ATTN + 2 * D_MODEL
RMS_EPS = 1e-6
NEG_INF = -1e30
ATTN_SCALE = HEAD_DIM ** -0.5
LOG2E = 1.4426950408889634

COL_U = 0
COL_ZP = D_POOL
COL_Q = 2 * D_POOL
COL_K = COL_Q + D_ATTN
COL_V = COL_K + D_ATTN
COL_ZA = COL_V + D_ATTN
COL_GP = COL_ZA + D_ATTN
COL_GA = COL_GP + D_MODEL

V7X_VMEM_LIMIT_BYTES = 60000 * 1024
COMPILER_SCRATCH_BYTES = 16 << 20

QROWS = 4
BAND_ROWS = 12
TQ = QROWS * GRID_W
TK = BAND_ROWS * GRID_W
HEADS_PER_STEP = 4
POOL_HALO = 64


def _params(block_bytes, n_axes):
    return pltpu.CompilerParams(
        dimension_semantics=("arbitrary",) * n_axes,
        vmem_limit_bytes=int(min(V7X_VMEM_LIMIT_BYTES, block_bytes + COMPILER_SCRATCH_BYTES)),
    )


def _sigmoid(x):
    return 1.0 / (1.0 + jnp.exp(-x))


def _split_bf16(v):
    hi = v.astype(BF16)
    lo = (v - hi.astype(F32)).astype(BF16)
    return hi, lo


def _ada_kernel(c_ref, w_ref, b_ref, o_ref):
    cv = c_ref[...]
    s_hi, s_lo = _split_bf16(cv * _sigmoid(cv))
    w_hi, w_lo = _split_bf16(w_ref[...])
    rows = s_hi.shape[0]
    r_hi = jnp.dot(jnp.concatenate([s_hi, s_lo], axis=0), w_hi, preferred_element_type=F32)
    r_lo = jnp.dot(s_hi, w_lo, preferred_element_type=F32)
    o_ref[...] = r_hi[:rows] + r_hi[rows:] + r_lo + b_ref[...]


def _ada_call(cvec, w_ada, b_ada):
    depth, d, n = w_ada.shape
    tn = 512
    blk = 2 * (d * tn * 4) + 2 * 8 * d * 4 + 4 * 8 * tn * 4
    return pl.pallas_call(
        _ada_kernel,
        grid=(depth, n // tn),
        in_specs=[
            pl.BlockSpec((8, d), lambda l, j: (0, 0)),
            pl.BlockSpec((None, d, tn), lambda l, j: (l, 0, j)),
            pl.BlockSpec((None, 1, tn), lambda l, j: (l, 0, j)),
        ],
        out_specs=pl.BlockSpec((None, 8, tn), lambda l, j: (l, 0, j)),
        out_shape=jax.ShapeDtypeStruct((depth, 8, n), F32),
        compiler_params=_params(blk, 2),
        name="ada",
    )(cvec, w_ada, b_ada.reshape(depth, 1, n))


def _mod_kernel(x_ref, g_ref, sh_ref, sc_ref, o_ref):
    x = x_ref[...]
    ms = jnp.mean(x * x, axis=-1, keepdims=True)
    y = x * lax.rsqrt(ms + RMS_EPS) * g_ref[...]
    o_ref[...] = (y * (1.0 + sc_ref[0]) + sh_ref[0]).astype(o_ref.dtype)


def _mod_call(x2, g, sh, sc, row_of_tile, tr):
    m, d = x2.shape
    vec = pl.BlockSpec((1, 1, d), lambda i: (row_of_tile(i), 0, 0))
    return pl.pallas_call(
        _mod_kernel,
        grid=(m // tr,),
        in_specs=[
            pl.BlockSpec((tr, d), lambda i: (i, 0)),
            pl.BlockSpec((1, d), lambda i: (0, 0)),
            vec, vec,
        ],
        out_specs=pl.BlockSpec((tr, d), lambda i: (i, 0)),
        out_shape=jax.ShapeDtypeStruct((m, d), BF16),
        compiler_params=_params(2 * tr * d * 6 + 3 * tr * d * 4, 1),
        name="modulate",
    )(x2, g.reshape(1, d), sh, sc)


def _rms_kernel(x_ref, g_ref, o_ref):
    x = x_ref[...]
    ms = jnp.mean(x * x, axis=-1, keepdims=True)
    o_ref[...] = x * lax.rsqrt(ms + RMS_EPS) * g_ref[...]


def _rms_call(x2, g, tr):
    m, d = x2.shape
    return pl.pallas_call(
        _rms_kernel,
        grid=(m // tr,),
        in_specs=[pl.BlockSpec((tr, d), lambda i: (i, 0)),
                  pl.BlockSpec((1, d), lambda i: (0, 0))],
        out_specs=pl.BlockSpec((tr, d), lambda i: (i, 0)),
        out_shape=jax.ShapeDtypeStruct((m, d), F32),
        compiler_params=_params(2 * tr * d * 8 + 2 * tr * d * 4, 1),
        name="final_norm",
    )(x2, g.reshape(1, d))


def _proj_kernel(a_ref, w_ref, b_ref, o_ref):
    acc = jnp.dot(a_ref[...], w_ref[...].astype(BF16), preferred_element_type=F32)
    o_ref[...] = (acc + b_ref[...]).astype(o_ref.dtype)


def _proj_call(a, w_stack, b_stack, layer, col0, n, tm, tn):
    m, k = a.shape
    depth, _, n_all = w_stack.shape
    jb = col0 // tn
    blk = tm * k * 2 + 2 * (k * tn * 4 + tm * tn * 2 + tn * 4)
    tmp = k * tn * 2
    return pl.pallas_call(
        _proj_kernel,
        grid=(m // tm, n // tn),
        in_specs=[
            pl.BlockSpec((tm, k), lambda i, j: (i, 0), pipeline_mode=pl.Buffered(1)),
            pl.BlockSpec((None, k, tn), lambda i, j: (layer, 0, jb + j)),
            pl.BlockSpec((None, 1, tn), lambda i, j: (layer, 0, jb + j)),
        ],
        out_specs=pl.BlockSpec((tm, tn), lambda i, j: (i, j)),
        out_shape=jax.ShapeDtypeStruct((m, n), BF16),
        compiler_params=_params(blk + tmp, 2),
        name="in_proj",
    )(a, w_stack, b_stack.reshape(depth, 1, n_all))


def _dot_nt(a, b):
    return lax.dot_general(a, b, (((1,), (1,)), ((), ())), preferred_element_type=F32)


def _scaled_q(q_bf16):
    return (q_bf16.astype(F32) * (ATTN_SCALE * LOG2E)).astype(BF16)


def _softmax_pv(s_loc, s_ctx, v_loc, v_ctx, out_dtype):
    mx = jnp.maximum(jnp.max(s_loc, axis=-1, keepdims=True), jnp.max(s_ctx, axis=-1, keepdims=True))
    e_loc = jnp.exp2(s_loc - mx)
    e_ctx = jnp.exp2(s_ctx - mx)
    den = jnp.sum(e_loc, axis=-1, keepdims=True) + jnp.sum(e_ctx, axis=-1, keepdims=True)
    o = (jnp.dot(e_loc.astype(BF16), v_loc, preferred_element_type=F32)
         + jnp.dot(e_ctx.astype(BF16), v_ctx, preferred_element_type=F32))
    return (o / den).astype(out_dtype)


def _na_kernel(q_ref, k_ref, v_ref, kc_ref, vc_ref, c2_ref, o_ref, tab_ref, *, n_groups, n_rows):
    g = pl.program_id(2)
    last = n_groups - 1
    kr0 = jnp.clip(QROWS * g - NA_KH // 2, 0, n_rows - BAND_ROWS)
    start = pl.multiple_of(kr0 * GRID_W, GRID_W)

    @pl.when(jnp.logical_or(g <= 1, g == last))
    def _():
        interior = jnp.logical_and(g > 0, g < last)
        lo_a = jnp.where(g == last, BAND_ROWS - NA_KH, 0)
        lo_b = jnp.where(interior, 1, 0)
        off = jnp.where(g == 0, NA_KH - 1,
                        jnp.where(g == last, NA_KH - 1 - BAND_ROWS + QROWS, NA_KH - 1 - NA_KH // 2))
        qrow = jnp.right_shift(lax.broadcasted_iota(jnp.int32, (TQ, 1), 0), LOG2_GRID_W)
        lo = lo_a + lo_b * qrow
        jrow = jnp.right_shift(lax.broadcasted_iota(jnp.int32, (1, TK), 1), LOG2_GRID_W)
        row_ok = jnp.logical_and(jrow >= lo, jrow < lo + NA_KH)
        for hh in range(HEADS_PER_STEP):
            bias = jnp.concatenate(
                [jnp.concatenate([c2_ref[hh, 2 * m - i + off + QROWS] for m in range(BAND_ROWS // 2)], axis=1)
                 for i in range(QROWS)], axis=0)
            tab_ref[hh] = jnp.where(row_ok, bias * LOG2E, NEG_INF)

    for hh in range(HEADS_PER_STEP):
        lanes = slice(hh * HEAD_DIM, (hh + 1) * HEAD_DIM)
        q = _scaled_q(q_ref[:, lanes])
        kb = k_ref[pl.ds(start, TK), lanes]
        vb = v_ref[pl.ds(start, TK), lanes]
        s_loc = _dot_nt(q, kb) + tab_ref[hh]
        s_ctx = _dot_nt(q, kc_ref[:, lanes])
        o_ref[:, lanes] = _softmax_pv(s_loc, s_ctx, vb, vc_ref[:, lanes], o_ref.dtype)


def _na_call(p, pc, c2, batch, seq, ctx_len, kc_col, vc_col):
    n_rows = seq // GRID_W
    n_groups = n_rows // QROWS
    wb = HEADS_PER_STEP * HEAD_DIM
    n_c2 = c2.shape[1]
    c2_bytes = HEADS_PER_STEP * n_c2 * GRID_W * 2 * GRID_W * 4
    tab_bytes = HEADS_PER_STEP * TQ * TK * 4
    blk = 2 * (TQ * wb * 2 * 2 + 2 * seq * wb * 2 + 2 * ctx_len * wb * 2 + c2_bytes)
    tmp = HEADS_PER_STEP * 4 * TQ * (TK + ctx_len) * 4
    kern = functools.partial(_na_kernel, n_groups=n_groups, n_rows=n_rows)
    return pl.pallas_call(
        kern,
        grid=(batch, N_HEADS // HEADS_PER_STEP, n_groups),
        in_specs=[
            pl.BlockSpec((TQ, wb), lambda b, h, g: (b * n_groups + g, COL_Q // wb + h)),
            pl.BlockSpec((seq, wb), lambda b, h, g: (b, COL_K // wb + h)),
            pl.BlockSpec((seq, wb), lambda b, h, g: (b, COL_V // wb + h)),
            pl.BlockSpec((ctx_len, wb), lambda b, h, g: (b, kc_col // wb + h)),
            pl.BlockSpec((ctx_len, wb), lambda b, h, g: (b, vc_col // wb + h)),
            pl.BlockSpec((HEADS_PER_STEP, n_c2, GRID_W, 2 * GRID_W), lambda b, h, g: (h, 0, 0, 0)),
        ],
        out_specs=pl.BlockSpec((TQ, wb), lambda b, h, g: (b * n_groups + g, h)),
        out_shape=jax.ShapeDtypeStruct((batch * seq, D_ATTN), BF16),
        scratch_shapes=[pltpu.VMEM((HEADS_PER_STEP, TQ, TK), F32)],
        compiler_params=_params(blk + tab_bytes + tmp, 3),
        name="na_attn",
    )(p, p, p, pc, pc, c2)


def _ctx_attn_kernel(q_ref, k_ref, v_ref, o_ref):
    s = _dot_nt(_scaled_q(q_ref[...]), k_ref[...])
    mx = jnp.max(s, axis=-1, keepdims=True)
    e = jnp.exp2(s - mx)
    den = jnp.sum(e, axis=-1, keepdims=True)
    o = jnp.dot(e.astype(BF16), v_ref[...], preferred_element_type=F32)
    o_ref[...] = (o / den).astype(o_ref.dtype)


def _ctx_attn_call(pc, batch, ctx_len):
    hb = HEAD_DIM
    return pl.pallas_call(
        _ctx_attn_kernel,
        grid=(batch, N_HEADS),
        in_specs=[
            pl.BlockSpec((ctx_len, hb), lambda b, h: (b, COL_Q // hb + h)),
            pl.BlockSpec((ctx_len, hb), lambda b, h: (b, COL_K // hb + h)),
            pl.BlockSpec((ctx_len, hb), lambda b, h: (b, COL_V // hb + h)),
        ],
        out_specs=pl.BlockSpec((ctx_len, hb), lambda b, h: (b, h)),
        out_shape=jax.ShapeDtypeStruct((batch * ctx_len, D_ATTN), BF16),
        compiler_params=_params(8 * ctx_len * hb * 2 + 8 * ctx_len * ctx_len * 4, 2),
        name="ctx_attn",
    )(pc, pc, pc)


def _bias_pair_table(rpb_l):
    qc = np.arange(GRID_W)[:, None]
    kc = np.arange(GRID_W)[None, :]
    ws = np.clip(qc - NA_KW // 2, 0, GRID_W - NA_KW)
    col_ok = (kc >= ws) & (kc < ws + NA_KW)
    dc = kc - qc + NA_KW - 1
    onehot = (dc[None] == np.arange(2 * NA_KW - 1)[:, None, None]) & col_ok[None]
    full = jnp.einsum("hdx,xqk->hdqk", rpb_l, jnp.asarray(onehot, F32), precision=lax.Precision.HIGHEST)
    full = full + jnp.asarray(np.where(col_ok, 0.0, NEG_INF), F32)
    n_e = BAND_ROWS + NA_KH - 1 + QROWS - 1
    n_dr = 2 * NA_KH - 1
    left = np.clip(np.arange(n_e) - QROWS, 0, n_dr - 1)
    right = np.clip(np.arange(n_e) - QROWS + 1, 0, n_dr - 1)
    return jnp.concatenate([full[:, left], full[:, right]], axis=-1)


def _pool_gate_kernel(ucur_ref, uprev_ref, unext_ref, zp_ref, za_ref, ya_ref, wp_ref, sp_ref, o_ref,
                      *, tiles_per_seq, seq):
    t = ucur_ref.shape[0]
    cg = D_POOL_GROUP
    base = (pl.program_id(0) % tiles_per_seq) * t
    ucat = jnp.concatenate([uprev_ref[...], ucur_ref[...], unext_ref[...]], axis=0)
    tc = t + 2 * POOL_HALO
    row = lax.broadcasted_iota(jnp.int32, (t, tc), 0) + base
    col = lax.broadcasted_iota(jnp.int32, (t, tc), 1) + (base - POOL_HALO)
    tpos = lax.broadcasted_iota(jnp.int32, (t, 1), 0) + base
    for gi, w in enumerate(POOL_WINDOWS):
        cols = slice(gi * cg, (gi + 1) * cg)
        lo = jnp.maximum(row - w // 2, 0)
        hi = jnp.minimum(row - w // 2 + w - 1, seq - 1)
        band = jnp.where(jnp.logical_and(col >= lo, col <= hi), 1.0, 0.0).astype(BF16)
        wsum = jnp.dot(band, ucat[:, cols], preferred_element_type=F32)
        cnt = (jnp.minimum(tpos - w // 2 + w - 1, seq - 1) - jnp.maximum(tpos - w // 2, 0) + 1).astype(F32)
        pooled = wsum / cnt - ucur_ref[:, cols].astype(F32)
        y = jnp.dot(pooled.astype(BF16), wp_ref[gi].astype(BF16), preferred_element_type=F32) * sp_ref[:, cols]
        z = zp_ref[:, cols].astype(F32)
        o_ref[:, cols] = (y * (z * _sigmoid(z))).astype(o_ref.dtype)
    za = za_ref[...].astype(F32)
    o_ref[:, D_POOL:] = (ya_ref[...].astype(F32) * (za * _sigmoid(za))).astype(o_ref.dtype)


def _pool_gate_call(p, y_attn, w_pool, s_pool_l, layer, seq, t):
    m = p.shape[0]
    tiles_per_seq = seq // t
    hpt = t // POOL_HALO
    n_halo_blocks = m // POOL_HALO
    wp_shape = w_pool.shape[1:]
    blk = 2 * (3 * t * D_POOL * 2 + 2 * POOL_HALO * D_POOL * 2 + t * D_ATTN * 2 + t * D_MODEL * 2
               + int(np.prod(wp_shape)) * 4 + D_POOL * 4)
    tmp = 4 * t * (t + 2 * POOL_HALO) * 4 + 8 * t * D_POOL_GROUP * 4
    kern = functools.partial(_pool_gate_kernel, tiles_per_seq=tiles_per_seq, seq=seq)
    return pl.pallas_call(
        kern,
        grid=(m // t,),
        in_specs=[
            pl.BlockSpec((t, D_POOL), lambda i: (i, COL_U // D_POOL)),
            pl.BlockSpec((POOL_HALO, D_POOL), lambda i: (jnp.maximum(i * hpt - 1, 0), COL_U // D_POOL)),
            pl.BlockSpec((POOL_HALO, D_POOL),
                         lambda i: (jnp.minimum((i + 1) * hpt, n_halo_blocks - 1), COL_U // D_POOL)),
            pl.BlockSpec((t, D_POOL), lambda i: (i, COL_ZP // D_POOL)),
            pl.BlockSpec((t, D_ATTN), lambda i: (i, COL_ZA // D_ATTN)),
            pl.BlockSpec((t, D_ATTN), lambda i: (i, 0)),
            pl.BlockSpec((None,) + wp_shape, lambda i: (layer, 0, 0, 0)),
            pl.BlockSpec((1, D_POOL), lambda i: (0, 0)),
        ],
        out_specs=pl.BlockSpec((t, D_MODEL), lambda i: (i, 0)),
        out_shape=jax.ShapeDtypeStruct((m, D_MODEL), BF16),
        compiler_params=_params(blk + tmp, 1),
        name="pool_gate",
    )(p, p, p, p, p, y_attn, w_pool, s_pool_l.reshape(1, D_POOL))


def _merge_kernel(a_ref, wp_ref, wa_ref, gp_ref, ga_ref, o_ref):
    br_p = jnp.dot(a_ref[:, :D_POOL], wp_ref[...].astype(BF16), preferred_element_type=F32)
    br_a = jnp.dot(a_ref[:, D_POOL:], wa_ref[...].astype(BF16), preferred_element_type=F32)
    gp = _sigmoid(gp_ref[...].astype(F32))
    ga = _sigmoid(ga_ref[...].astype(F32))
    o_ref[...] = (gp * br_p + ga * br_a).astype(o_ref.dtype)


def _merge_call(a, w_br_pool, w_br_attn, p, layer, tm, tn):
    m = a.shape[0]
    n = D_MODEL
    blk = 2 * (tm * D_MODEL * 2 + 2 * D_POOL * tn * 4 + 3 * tm * tn * 2)
    tmp = 2 * tm * tn * 4
    return pl.pallas_call(
        _merge_kernel,
        grid=(m // tm, n // tn),
        in_specs=[
            pl.BlockSpec((tm, D_MODEL), lambda i, j: (i, 0)),
            pl.BlockSpec((None, D_POOL, tn), lambda i, j: (layer, 0, j)),
            pl.BlockSpec((None, D_ATTN, tn), lambda i, j: (layer, 0, j)),
            pl.BlockSpec((tm, tn), lambda i, j: (i, COL_GP // tn + j)),
            pl.BlockSpec((tm, tn), lambda i, j: (i, COL_GA // tn + j)),
        ],
        out_specs=pl.BlockSpec((tm, tn), lambda i, j: (i, j)),
        out_shape=jax.ShapeDtypeStruct((m, n), BF16),
        compiler_params=_params(blk + tmp, 2),
        name="merge",
    )(a, w_br_pool, w_br_attn, p, p)


def _out_kernel(m_ref, w_ref, x_ref, gt_ref, o_ref):
    acc = jnp.dot(m_ref[...], w_ref[...].astype(BF16), preferred_element_type=F32)
    o_ref[...] = x_ref[...] + gt_ref[0] * acc


def _out_call(mix, w_out, x2, gt, layer, row_of_tile, tm, tn):
    m, k = mix.shape
    n = w_out.shape[2]
    blk = tm * k * 2 + 2 * (k * tn * 4 + 2 * tm * tn * 4 + tn * 4)
    tmp = tm * tn * 4
    return pl.pallas_call(
        _out_kernel,
        grid=(m // tm, n // tn),
        in_specs=[
            pl.BlockSpec((tm, k), lambda i, j: (i, 0), pipeline_mode=pl.Buffered(1)),
            pl.BlockSpec((None, k, tn), lambda i, j: (layer, 0, j)),
            pl.BlockSpec((tm, tn), lambda i, j: (i, j)),
            pl.BlockSpec((1, 1, tn), lambda i, j: (row_of_tile(i), 0, j)),
        ],
        out_specs=pl.BlockSpec((tm, tn), lambda i, j: (i, j)),
        out_shape=jax.ShapeDtypeStruct((m, n), F32),
        compiler_params=_params(blk + tmp, 2),
        name="out_proj",
    )(mix, w_out, x2, gt)


def kernel(x, c, ctx, c_ctx, norm_g, w_ada, b_ada, w_in, b_in, w_pool, s_pool, rpb,
           w_br_pool, w_br_attn, w_out, final_g):
    batch, seq, d = x.shape
    ctx_len = ctx.shape[1]
    depth = w_in.shape[0]
    ctx_row = batch

    x_lat = x.reshape(batch * seq, d)
    x_ctx = ctx.reshape(batch * ctx_len, d)
    cvec = jnp.zeros((8, d), F32).at[:batch].set(c).at[ctx_row].set(c_ctx)
    ada = _ada_call(cvec, w_ada, b_ada)

    tm_in, tn_in = 2048, 512
    tm_out, tn_out = 2048, 512
    tm_mrg, tn_mrg = 1024, 512
    tm_ctx = batch * ctx_len
    t_row = 256
    lat_row = lambda tile_rows: (lambda i: i // (seq // tile_rows))
    ctx_row_fn = lambda i: ctx_row

    for l in range(depth):
        last = l == depth - 1
        mod = ada[l].reshape(8, 3, 1, d)
        sh, sc, gt = mod[:, 0], mod[:, 1], mod[:, 2]
        c2 = _bias_pair_table(rpb[l])

        h_lat = _mod_call(x_lat, norm_g[l], sh, sc, lat_row(t_row), t_row)
        h_ctx = _mod_call(x_ctx, norm_g[l], sh, sc, ctx_row_fn, t_row)
        p_lat = _proj_call(h_lat, w_in, b_in, l, 0, D_IN, tm_in, tn_in)
        if last:
            p_ctx = _proj_call(h_ctx, w_in, b_in, l, COL_K, 2 * D_ATTN, tm_ctx, tn_in)
            kc_col, vc_col = 0, D_ATTN
        else:
            p_ctx = _proj_call(h_ctx, w_in, b_in, l, 0, D_IN, tm_ctx, tn_in)
            kc_col, vc_col = COL_K, COL_V

        y_attn = _na_call(p_lat, p_ctx, c2, batch, seq, ctx_len, kc_col, vc_col)
        a_lat = _pool_gate_call(p_lat, y_attn, w_pool, s_pool[l], l, seq, t_row)
        mix = _merge_call(a_lat, w_br_pool, w_br_attn, p_lat, l, tm_mrg, tn_mrg)
        x_lat_new = _out_call(mix, w_out, x_lat, gt, l, lat_row(tm_out), tm_out, tn_out)

        if not last:
            y_attn_c = _ctx_attn_call(p_ctx, batch, ctx_len)
            a_ctx = _pool_gate_call(p_ctx, y_attn_c, w_pool, s_pool[l], l, ctx_len, t_row)
            mix_c = _merge_call(a_ctx, w_br_pool, w_br_attn, p_ctx, l, tm_ctx, tn_mrg)
            x_ctx = _out_call(mix_c, w_out, x_ctx, gt, l, ctx_row_fn, tm_ctx, tn_out)
        x_lat = x_lat_new

    return _rms_call(x_lat, final_g, t_row).reshape(batch, seq, d)
```

```python
import functools

import numpy as np
import jax
import jax.numpy as jnp
from jax import lax
from jax.experimental import pallas as pl
from jax.experimental.pallas import tpu as pltpu

F32 = jnp.float32
BF16 = jnp.bfloat16

D_MODEL = 4096
GRID_W = 64
LOG2_GRID_W = 6
D_POOL = D_MODEL // 2
POOL_WINDOWS = (2, 4, 8, 16)
D_POOL_GROUP = D_POOL // len(POOL_WINDOWS)
HEAD_DIM = 128
D_ATTN = D_MODEL // 2
N_HEADS = D_ATTN // HEAD_DIM
NA_KH = 8
NA_KW = 16
D_IN = 2 * D_POOL + 4 * D_ATTN + 2 * D_MODEL
RMS_EPS = 1e-6
NEG_INF = -1e30
ATTN_SCALE = HEAD_DIM ** -0.5
LOG2E = 1.4426950408889634

COL_U = 0
COL_ZP = D_POOL
COL_Q = 2 * D_POOL
COL_K = COL_Q + D_ATTN
COL_V = COL_K + D_ATTN
COL_ZA = COL_V + D_ATTN
COL_GP = COL_ZA + D_ATTN
COL_GA = COL_GP + D_MODEL

V7X_VMEM_LIMIT_BYTES = 60000 * 1024
COMPILER_SCRATCH_BYTES = 16 << 20

QROWS = 4
BAND_ROWS = 12
TQ = QROWS * GRID_W
TK = BAND_ROWS * GRID_W
HEADS_PER_STEP = 4
POOL_HALO = 64


def _params(block_bytes, n_axes):
    return pltpu.CompilerParams(
        dimension_semantics=("arbitrary",) * n_axes,
        vmem_limit_bytes=int(min(V7X_VMEM_LIMIT_BYTES, block_bytes + COMPILER_SCRATCH_BYTES)),
    )


def _sigmoid(x):
    return 1.0 / (1.0 + jnp.exp(-x))


def _silu_bf16(z):
    hz = z * 0.5
    return hz + hz * jnp.tanh(hz)


def _split_bf16(v):
    hi = v.astype(BF16)
    lo = (v - hi.astype(F32)).astype(BF16)
    return hi, lo


def _ada_kernel(c_ref, w_ref, b_ref, o_ref):
    cv = c_ref[...]
    s_hi, s_lo = _split_bf16(cv * _sigmoid(cv))
    w_hi, w_lo = _split_bf16(w_ref[...])
    rows = s_hi.shape[0]
    r_hi = jnp.dot(jnp.concatenate([s_hi, s_lo], axis=0), w_hi, preferred_element_type=F32)
    r_lo = jnp.dot(s_hi, w_lo, preferred_element_type=F32)
    o_ref[...] = r_hi[:rows] + r_hi[rows:] + r_lo + b_ref[...]


def _ada_call(cvec, w_ada, b_ada):
    depth, d, n = w_ada.shape
    tn = 512
    blk = 2 * (d * tn * 4) + 2 * 8 * d * 4 + 4 * 8 * tn * 4
    return pl.pallas_call(
        _ada_kernel,
        grid=(depth, n // tn),
        in_specs=[
            pl.BlockSpec((8, d), lambda l, j: (0, 0)),
            pl.BlockSpec((None, d, tn), lambda l, j: (l, 0, j)),
            pl.BlockSpec((None, 1, tn), lambda l, j: (l, 0, j)),
        ],
        out_specs=pl.BlockSpec((None, 8, tn), lambda l, j: (l, 0, j)),
        out_shape=jax.ShapeDtypeStruct((depth, 8, n), F32),
        compiler_params=_params(blk, 2),
        name="ada",
    )(cvec, w_ada, b_ada.reshape(depth, 1, n))


def _mod_kernel(x_ref, g_ref, sh_ref, sc_ref, o_ref):
    x = x_ref[...]
    ms = jnp.mean(x * x, axis=-1, keepdims=True)
    y = x * lax.rsqrt(ms + RMS_EPS) * g_ref[...]
    o_ref[...] = (y * (1.0 + sc_ref[0]) + sh_ref[0]).astype(o_ref.dtype)


def _mod_call(x2, g, sh, sc, row_of_tile, tr):
    m, d = x2.shape
    vec = pl.BlockSpec((1, 1, d), lambda i: (row_of_tile(i), 0, 0))
    return pl.pallas_call(
        _mod_kernel,
        grid=(m // tr,),
        in_specs=[
            pl.BlockSpec((tr, d), lambda i: (i, 0)),
            pl.BlockSpec((1, d), lambda i: (0, 0)),
            vec, vec,
        ],
        out_specs=pl.BlockSpec((tr, d), lambda i: (i, 0)),
        out_shape=jax.ShapeDtypeStruct((m, d), BF16),
        compiler_params=_params(2 * tr * d * 6 + 3 * tr * d * 4, 1),
        name="modulate",
    )(x2, g.reshape(1, d), sh, sc)


def _rms_kernel(x_ref, g_ref, o_ref):
    x = x_ref[...]
    ms = jnp.mean(x * x, axis=-1, keepdims=True)
    o_ref[...] = x * lax.rsqrt(ms + RMS_EPS) * g_ref[...]


def _rms_call(x2, g, tr):
    m, d = x2.shape
    return pl.pallas_call(
        _rms_kernel,
        grid=(m // tr,),
        in_specs=[pl.BlockSpec((tr, d), lambda i: (i, 0)),
                  pl.BlockSpec((1, d), lambda i: (0, 0))],
        out_specs=pl.BlockSpec((tr, d), lambda i: (i, 0)),
        out_shape=jax.ShapeDtypeStruct((m, d), F32),
        compiler_params=_params(2 * tr * d * 8 + 2 * tr * d * 4, 1),
        name="final_norm",
    )(x2, g.reshape(1, d))


def _proj_kernel(a_ref, w_ref, b_ref, o_ref):
    acc = jnp.dot(a_ref[...], w_ref[...].astype(BF16), preferred_element_type=F32)
    o_ref[...] = (acc + b_ref[...]).astype(o_ref.dtype)


def _proj_call(a, w_stack, b_stack, layer, col0, n, tm, tn):
    m, k = a.shape
    depth, _, n_all = w_stack.shape
    jb = col0 // tn
    blk = tm * k * 2 + 2 * (k * tn * 4 + tm * tn * 2 + tn * 4)
    tmp = k * tn * 2
    return pl.pallas_call(
        _proj_kernel,
        grid=(m // tm, n // tn),
        in_specs=[
            pl.BlockSpec((tm, k), lambda i, j: (i, 0), pipeline_mode=pl.Buffered(1)),
            pl.BlockSpec((None, k, tn), lambda i, j: (layer, 0, jb + j)),
            pl.BlockSpec((None, 1, tn), lambda i, j: (layer, 0, jb + j)),
        ],
        out_specs=pl.BlockSpec((tm, tn), lambda i, j: (i, j)),
        out_shape=jax.ShapeDtypeStruct((m, n), BF16),
        compiler_params=_params(blk + tmp, 2),
        name="in_proj",
    )(a, w_stack, b_stack.reshape(depth, 1, n_all))


def _dot_nt(a, b):
    return lax.dot_general(a, b, (((1,), (1,)), ((), ())), preferred_element_type=F32)


def _scaled_q(q_bf16):
    return (q_bf16.astype(F32) * (ATTN_SCALE * LOG2E)).astype(BF16)


def _with_ones(v):
    return jnp.concatenate([v, jnp.ones_like(v)], axis=1)


def _softmax_pv(s_loc, s_ctx, v_loc, v_ctx, out_dtype):
    mx = jnp.maximum(jnp.max(s_loc, axis=-1, keepdims=True), jnp.max(s_ctx, axis=-1, keepdims=True))
    e_loc = jnp.exp2(s_loc - mx).astype(BF16)
    e_ctx = jnp.exp2(s_ctx - mx).astype(BF16)
    acc = (jnp.dot(e_loc, _with_ones(v_loc), preferred_element_type=F32)
           + jnp.dot(e_ctx, _with_ones(v_ctx), preferred_element_type=F32))
    return (acc[:, :HEAD_DIM] / acc[:, HEAD_DIM:]).astype(out_dtype)


def _band_start(g, n_rows):
    kr0 = jnp.clip(QROWS * g - NA_KH // 2, 0, n_rows - BAND_ROWS)
    return pl.multiple_of(kr0 * GRID_W, GRID_W)


def _na_scores(q_ref, k_ref, kc_ref, tab_ref, dst_ref, start):
    for hh in range(HEADS_PER_STEP):
        lanes = slice(hh * HEAD_DIM, (hh + 1) * HEAD_DIM)
        q = _scaled_q(q_ref[:, lanes])
        dst_ref[hh, :, :TK] = _dot_nt(q, k_ref[pl.ds(start, TK), lanes]) + tab_ref[hh]
        dst_ref[hh, :, TK:] = _dot_nt(q, kc_ref[:, lanes])


def _na_kernel(q_ref, k_ref, v_ref, kc_ref, vc_ref, c2_ref, o_ref, tab_ref, sa_ref, sb_ref, *, n_groups, n_rows):
    step_id = pl.program_id(2)
    last = n_groups - 1
    g = jnp.minimum(step_id, last)

    @pl.when(jnp.logical_or(g <= 1, g == last))
    def _():
        interior = jnp.logical_and(g > 0, g < last)
        lo_a = jnp.where(g == last, BAND_ROWS - NA_KH, 0)
        lo_b = jnp.where(interior, 1, 0)
        off = jnp.where(g == 0, NA_KH - 1,
                        jnp.where(g == last, NA_KH - 1 - BAND_ROWS + QROWS, NA_KH - 1 - NA_KH // 2))
        qrow = jnp.right_shift(lax.broadcasted_iota(jnp.int32, (TQ, 1), 0), LOG2_GRID_W)
        lo = lo_a + lo_b * qrow
        jrow = jnp.right_shift(lax.broadcasted_iota(jnp.int32, (1, TK), 1), LOG2_GRID_W)
        row_ok = jnp.logical_and(jrow >= lo, jrow < lo + NA_KH)
        for hh in range(HEADS_PER_STEP):
            bias = jnp.concatenate(
                [jnp.concatenate([c2_ref[hh, 2 * m - i + off + QROWS] for m in range(BAND_ROWS // 2)], axis=1)
                 for i in range(QROWS)], axis=0)
            tab_ref[hh] = jnp.where(row_ok, bias * LOG2E, NEG_INF)

    @pl.when(step_id == 0)
    def _():
        _na_scores(q_ref, k_ref, kc_ref, tab_ref, sb_ref, _band_start(0, n_rows))

    kstart = _band_start(g, n_rows)
    vstart = _band_start(jnp.maximum(step_id - 1, 0), n_rows)

    def step(src_ref, dst_ref):
        _na_scores(q_ref, k_ref, kc_ref, tab_ref, dst_ref, kstart)
        for hh in range(HEADS_PER_STEP):
            lanes = slice(hh * HEAD_DIM, (hh + 1) * HEAD_DIM)
            o_ref[:, lanes] = _softmax_pv(src_ref[hh, :, :TK], src_ref[hh, :, TK:],
                                          v_ref[pl.ds(vstart, TK), lanes], vc_ref[:, lanes], o_ref.dtype)

    @pl.when(step_id % 2 == 0)
    def _():
        step(sb_ref, sa_ref)

    @pl.when(step_id % 2 == 1)
    def _():
        step(sa_ref, sb_ref)


def _na_call(p, pc, c2, layer, batch, seq, ctx_len, kc_col, vc_col):
    n_rows = seq // GRID_W
    n_groups = n_rows // QROWS
    last = n_groups - 1
    wb = HEADS_PER_STEP * HEAD_DIM
    n_c2 = c2.shape[2]
    c2_bytes = HEADS_PER_STEP * n_c2 * GRID_W * 2 * GRID_W * 4
    tab_bytes = HEADS_PER_STEP * TQ * TK * 4
    score_bytes = HEADS_PER_STEP * TQ * (TK + ctx_len) * 4
    blk = 2 * (TQ * wb * 2 * 2 + 2 * seq * wb * 2 + 2 * ctx_len * wb * 2 + c2_bytes)
    kern = functools.partial(_na_kernel, n_groups=n_groups, n_rows=n_rows)
    return pl.pallas_call(
        kern,
        grid=(batch, N_HEADS // HEADS_PER_STEP, n_groups + 1),
        in_specs=[
            pl.BlockSpec((TQ, wb), lambda b, h, s: (b * n_groups + jnp.minimum(s, last), COL_Q // wb + h)),
            pl.BlockSpec((seq, wb), lambda b, h, s: (b, COL_K // wb + h)),
            pl.BlockSpec((seq, wb), lambda b, h, s: (b, COL_V // wb + h)),
            pl.BlockSpec((ctx_len, wb), lambda b, h, s: (b, kc_col // wb + h)),
            pl.BlockSpec((ctx_len, wb), lambda b, h, s: (b, vc_col // wb + h)),
            pl.BlockSpec((None, HEADS_PER_STEP, n_c2, GRID_W, 2 * GRID_W), lambda b, h, s: (layer, h, 0, 0, 0)),
        ],
        out_specs=pl.BlockSpec((TQ, wb), lambda b, h, s: (b * n_groups + jnp.maximum(s - 1, 0), h)),
        out_shape=jax.ShapeDtypeStruct((batch * seq, D_ATTN), BF16),
        scratch_shapes=[pltpu.VMEM((HEADS_PER_STEP, TQ, TK), F32),
                        pltpu.VMEM((HEADS_PER_STEP, TQ, TK + ctx_len), F32),
                        pltpu.VMEM((HEADS_PER_STEP, TQ, TK + ctx_len), F32)],
        compiler_params=_params(blk + tab_bytes + 2 * score_bytes, 3),
        name="na_attn",
    )(p, p, p, pc, pc, c2)


def _ctx_attn_kernel(q_ref, k_ref, v_ref, o_ref):
    s = _dot_nt(_scaled_q(q_ref[...]), k_ref[...])
    mx = jnp.max(s, axis=-1, keepdims=True)
    e = jnp.exp2(s - mx)
    den = jnp.sum(e, axis=-1, keepdims=True)
    o = jnp.dot(e.astype(BF16), v_ref[...], preferred_element_type=F32)
    o_ref[...] = (o / den).astype(o_ref.dtype)


def _ctx_attn_call(pc, batch, ctx_len):
    hb = HEAD_DIM
    return pl.pallas_call(
        _ctx_attn_kernel,
        grid=(batch, N_HEADS),
        in_specs=[
            pl.BlockSpec((ctx_len, hb), lambda b, h: (b, COL_Q // hb + h)),
            pl.BlockSpec((ctx_len, hb), lambda b, h: (b, COL_K // hb + h)),
            pl.BlockSpec((ctx_len, hb), lambda b, h: (b, COL_V // hb + h)),
        ],
        out_specs=pl.BlockSpec((ctx_len, hb), lambda b, h: (b, h)),
        out_shape=jax.ShapeDtypeStruct((batch * ctx_len, D_ATTN), BF16),
        compiler_params=_params(8 * ctx_len * hb * 2 + 8 * ctx_len * ctx_len * 4, 2),
        name="ctx_attn",
    )(pc, pc, pc)


def _bias_pair_table(rpb):
    qc = np.arange(GRID_W)[:, None]
    kc = np.arange(GRID_W)[None, :]
    ws = np.clip(qc - NA_KW // 2, 0, GRID_W - NA_KW)
    col_ok = (kc >= ws) & (kc < ws + NA_KW)
    dc = kc - qc + NA_KW - 1
    onehot = (dc[None] == np.arange(2 * NA_KW - 1)[:, None, None]) & col_ok[None]
    full = jnp.einsum("lhdx,xqk->lhdqk", rpb, jnp.asarray(onehot, F32), precision=lax.Precision.HIGHEST)
    full = full + jnp.asarray(np.where(col_ok, 0.0, NEG_INF), F32)
    n_e = BAND_ROWS + NA_KH - 1 + QROWS - 1
    extra = n_e - (2 * NA_KH - 1)
    edge = lambda lo: jnp.pad(full, ((0, 0), (0, 0), (lo, extra - lo), (0, 0), (0, 0)), mode="edge")
    return jnp.concatenate([edge(QROWS), edge(QROWS - 1)], axis=-1)


def _pool_gate_kernel(ucur_ref, uprev_ref, unext_ref, zp_ref, za_ref, ya_ref, wp_ref, sp_ref, o_ref,
                      *, tiles_per_seq, seq):
    t = ucur_ref.shape[0]
    cg = D_POOL_GROUP
    base = (pl.program_id(0) % tiles_per_seq) * t
    ucat = jnp.concatenate([uprev_ref[...], ucur_ref[...], unext_ref[...]], axis=0)
    tc = t + 2 * POOL_HALO
    row = lax.broadcasted_iota(jnp.int32, (t, tc), 0) + base
    col = lax.broadcasted_iota(jnp.int32, (t, tc), 1) + (base - POOL_HALO)
    tpos = lax.broadcasted_iota(jnp.int32, (t, 1), 0) + base
    for gi, w in enumerate(POOL_WINDOWS):
        cols = slice(gi * cg, (gi + 1) * cg)
        lo = jnp.maximum(row - w // 2, 0)
        hi = jnp.minimum(row - w // 2 + w - 1, seq - 1)
        band = jnp.where(jnp.logical_and(col >= lo, col <= hi), 1.0, 0.0).astype(BF16)
        wsum = jnp.dot(band, ucat[:, cols], preferred_element_type=F32)
        cnt = (jnp.minimum(tpos - w // 2 + w - 1, seq - 1) - jnp.maximum(tpos - w // 2, 0) + 1).astype(F32)
        pooled = wsum / cnt - ucur_ref[:, cols].astype(F32)
        y = jnp.dot(pooled.astype(BF16), wp_ref[gi].astype(BF16), preferred_element_type=F32) * sp_ref[:, cols]
        o_ref[:, cols] = y.astype(BF16) * _silu_bf16(zp_ref[:, cols])
    o_ref[:, D_POOL:] = ya_ref[...] * _silu_bf16(za_ref[...])


def _pool_gate_call(p, y_attn, w_pool, s_pool_l, layer, seq, t):
    m = p.shape[0]
    tiles_per_seq = seq // t
    hpt = t // POOL_HALO
    n_halo_blocks = m // POOL_HALO
    wp_shape = w_pool.shape[1:]
    blk = 2 * (3 * t * D_POOL * 2 + 2 * POOL_HALO * D_POOL * 2 + t * D_ATTN * 2 + t * D_MODEL * 2
               + int(np.prod(wp_shape)) * 4 + D_POOL * 4)
    tmp = 4 * t * (t + 2 * POOL_HALO) * 4 + 8 * t * D_POOL_GROUP * 4
    kern = functools.partial(_pool_gate_kernel, tiles_per_seq=tiles_per_seq, seq=seq)
    return pl.pallas_call(
        kern,
        grid=(m // t,),
        in_specs=[
            pl.BlockSpec((t, D_POOL), lambda i: (i, COL_U // D_POOL)),
            pl.BlockSpec((POOL_HALO, D_POOL), lambda i: (jnp.maximum(i * hpt - 1, 0), COL_U // D_POOL)),
            pl.BlockSpec((POOL_HALO, D_POOL),
                         lambda i: (jnp.minimum((i + 1) * hpt, n_halo_blocks - 1), COL_U // D_POOL)),
            pl.BlockSpec((t, D_POOL), lambda i: (i, COL_ZP // D_POOL)),
            pl.BlockSpec((t, D_ATTN), lambda i: (i, COL_ZA // D_ATTN)),
            pl.BlockSpec((t, D_ATTN), lambda i: (i, 0)),
            pl.BlockSpec((None,) + wp_shape, lambda i: (layer, 0, 0, 0)),
            pl.BlockSpec((1, D_POOL), lambda i: (0, 0)),
        ],
        out_specs=pl.BlockSpec((t, D_MODEL), lambda i: (i, 0)),
        out_shape=jax.ShapeDtypeStruct((m, D_MODEL), BF16),
        compiler_params=_params(blk + tmp, 1),
        name="pool_gate",
    )(p, p, p, p, p, y_attn, w_pool, s_pool_l.reshape(1, D_POOL))


def _merge_kernel(a_ref, wp_ref, wa_ref, gp_ref, ga_ref, o_ref):
    br_p = jnp.dot(a_ref[:, :D_POOL], wp_ref[...].astype(BF16), preferred_element_type=F32)
    br_a = jnp.dot(a_ref[:, D_POOL:], wa_ref[...].astype(BF16), preferred_element_type=F32)
    gp = _sigmoid(gp_ref[...].astype(F32))
    ga = _sigmoid(ga_ref[...].astype(F32))
    o_ref[...] = (gp * br_p + ga * br_a).astype(o_ref.dtype)


def _merge_call(a, w_br_pool, w_br_attn, p, layer, tm, tn):
    m = a.shape[0]
    n = D_MODEL
    blk = 2 * (tm * D_MODEL * 2 + 2 * D_POOL * tn * 4 + 3 * tm * tn * 2)
    tmp = 2 * tm * tn * 4
    return pl.pallas_call(
        _merge_kernel,
        grid=(m // tm, n // tn),
        in_specs=[
            pl.BlockSpec((tm, D_MODEL), lambda i, j: (i, 0)),
            pl.BlockSpec((None, D_POOL, tn), lambda i, j: (layer, 0, j)),
            pl.BlockSpec((None, D_ATTN, tn), lambda i, j: (layer, 0, j)),
            pl.BlockSpec((tm, tn), lambda i, j: (i, COL_GP // tn + j)),
            pl.BlockSpec((tm, tn), lambda i, j: (i, COL_GA // tn + j)),
        ],
        out_specs=pl.BlockSpec((tm, tn), lambda i, j: (i, j)),
        out_shape=jax.ShapeDtypeStruct((m, n), BF16),
        compiler_params=_params(blk + tmp, 2),
        name="merge",
    )(a, w_br_pool, w_br_attn, p, p)


def _out_kernel(m_ref, w_ref, x_ref, gt_ref, o_ref):
    acc = jnp.dot(m_ref[...], w_ref[...].astype(BF16), preferred_element_type=F32)
    o_ref[...] = x_ref[...] + gt_ref[0] * acc


def _out_call(mix, w_out, x2, gt, layer, row_of_tile, tm, tn):
    m, k = mix.shape
    n = w_out.shape[2]
    blk = tm * k * 2 + 2 * (k * tn * 4 + 2 * tm * tn * 4 + tn * 4)
    tmp = tm * tn * 4
    return pl.pallas_call(
        _out_kernel,
        grid=(m // tm, n // tn),
        in_specs=[
            pl.BlockSpec((tm, k), lambda i, j: (i, 0), pipeline_mode=pl.Buffered(1)),
            pl.BlockSpec((None, k, tn), lambda i, j: (layer, 0, j)),
            pl.BlockSpec((tm, tn), lambda i, j: (i, j)),
            pl.BlockSpec((1, 1, tn), lambda i, j: (row_of_tile(i), 0, j)),
        ],
        out_specs=pl.BlockSpec((tm, tn), lambda i, j: (i, j)),
        out_shape=jax.ShapeDtypeStruct((m, n), F32),
        compiler_params=_params(blk + tmp, 2),
        name="out_proj",
    )(mix, w_out, x2, gt)


def kernel(x, c, ctx, c_ctx, norm_g, w_ada, b_ada, w_in, b_in, w_pool, s_pool, rpb,
           w_br_pool, w_br_attn, w_out, final_g):
    batch, seq, d = x.shape
    ctx_len = ctx.shape[1]
    depth = w_in.shape[0]
    ctx_row = batch

    x_lat = x.reshape(batch * seq, d)
    x_ctx = ctx.reshape(batch * ctx_len, d)
    cvec = jnp.zeros((8, d), F32).at[:batch].set(c).at[ctx_row].set(c_ctx)
    ada = _ada_call(cvec, w_ada, b_ada)
    c2 = _bias_pair_table(rpb)

    tm_in, tn_in = 2048, 512
    tm_out, tn_out = 2048, 512
    tm_mrg, tn_mrg = 1024, 512
    tm_ctx = batch * ctx_len
    t_row = 256
    lat_row = lambda tile_rows: (lambda i: i // (seq // tile_rows))
    ctx_row_fn = lambda i: ctx_row

    for l in range(depth):
        last = l == depth - 1
        mod = ada[l].reshape(8, 3, 1, d)
        sh, sc, gt = mod[:, 0], mod[:, 1], mod[:, 2]

        h_lat = _mod_call(x_lat, norm_g[l], sh, sc, lat_row(t_row), t_row)
        h_ctx = _mod_call(x_ctx, norm_g[l], sh, sc, ctx_row_fn, t_row)
        p_lat = _proj_call(h_lat, w_in, b_in, l, 0, D_IN, tm_in, tn_in)
        if last:
            p_ctx = _proj_call(h_ctx, w_in, b_in, l, COL_K, 2 * D_ATTN, tm_ctx, tn_in)
            kc_col, vc_col = 0, D_ATTN
        else:
            p_ctx = _proj_call(h_ctx, w_in, b_in, l, 0, D_IN, tm_ctx, tn_in)
            kc_col, vc_col = COL_K, COL_V

        y_attn = _na_call(p_lat, p_ctx, c2, l, batch, seq, ctx_len, kc_col, vc_col)
        a_lat = _pool_gate_call(p_lat, y_attn, w_pool, s_pool[l], l, seq, t_row)
        mix = _merge_call(a_lat, w_br_pool, w_br_attn, p_lat, l, tm_mrg, tn_mrg)
        x_lat_new = _out_call(mix, w_out, x_lat, gt, l, lat_row(tm_out), tm_out, tn_out)

        if not last:
            y_attn_c = _ctx_attn_call(p_ctx, batch, ctx_len)
            a_ctx = _pool_gate_call(p_ctx, y_attn_c, w_pool, s_pool[l], l, ctx_len, t_row)
            mix_c = _merge_call(a_ctx, w_br_pool, w_br_attn, p_ctx, l, tm_ctx, tn_mrg)
            x_ctx = _out_call(mix_c, w_out, x_ctx, gt, l, ctx_row_fn, tm_ctx, tn_out)
        x_lat = x_lat_new

    return _rms_call(x_lat, final_g, t_row).reshape(batch, seq, d)
```

```python
import functools

import numpy as np
import jax
import jax.numpy as jnp
from jax import lax
from jax.experimental import pallas as pl
from jax.experimental.pallas import tpu as pltpu

F32 = jnp.float32
BF16 = jnp.bfloat16

D_MODEL = 4096
GRID_W = 64
LOG2_GRID_W = 6
D_POOL = D_MODEL // 2
POOL_WINDOWS = (2, 4, 8, 16)
D_POOL_GROUP = D_POOL // len(POOL_WINDOWS)
HEAD_DIM = 128
D_ATTN = D_MODEL // 2
N_HEADS = D_ATTN // HEAD_DIM
NA_KH = 8
NA_KW = 16
D_IN = 2 * D_POOL + 4 * D_ATTN + 2 * D_MODEL
RMS_EPS = 1e-6
NEG_INF = -1e30
ATTN_SCALE = HEAD_DIM ** -0.5
LOG2E = 1.4426950408889634

COL_U = 0
COL_ZP = D_POOL
COL_Q = 2 * D_POOL
COL_K = COL_Q + D_ATTN
COL_V = COL_K + D_ATTN
COL_ZA = COL_V + D_ATTN
COL_GP = COL_ZA + D_ATTN
COL_GA = COL_GP + D_MODEL

V7X_VMEM_LIMIT_BYTES = 60000 * 1024
COMPILER_SCRATCH_BYTES = 16 << 20
MATMUL_SPILL_BYTES = 4 << 20

QROWS = 4
BAND_ROWS = 12
TQ = QROWS * GRID_W
TK = BAND_ROWS * GRID_W
HEADS_PER_STEP = 4
POOL_HALO = 64


def _params(block_bytes, n_axes):
    return pltpu.CompilerParams(
        dimension_semantics=("arbitrary",) * n_axes,
        vmem_limit_bytes=int(min(V7X_VMEM_LIMIT_BYTES, block_bytes + COMPILER_SCRATCH_BYTES)),
    )


def _sigmoid(x):
    return 1.0 / (1.0 + jnp.exp(-x))


def _silu_bf16(z):
    hz = z * 0.5
    return hz + hz * jnp.tanh(hz)


def _split_bf16(v):
    hi = v.astype(BF16)
    lo = (v - hi.astype(F32)).astype(BF16)
    return hi, lo


def _ada_kernel(c_ref, w_ref, b_ref, o_ref):
    cv = c_ref[...]
    s_hi, s_lo = _split_bf16(cv * _sigmoid(cv))
    w_hi, w_lo = _split_bf16(w_ref[...])
    rows = s_hi.shape[0]
    r_hi = jnp.dot(jnp.concatenate([s_hi, s_lo], axis=0), w_hi, preferred_element_type=F32)
    r_lo = jnp.dot(s_hi, w_lo, preferred_element_type=F32)
    o_ref[...] = r_hi[:rows] + r_hi[rows:] + r_lo + b_ref[...]


def _ada_call(cvec, w_ada, b_ada):
    depth, d, n = w_ada.shape
    tn = 512
    blk = 2 * (d * tn * 4) + 2 * 8 * d * 4 + 4 * 8 * tn * 4
    return pl.pallas_call(
        _ada_kernel,
        grid=(depth, n // tn),
        in_specs=[
            pl.BlockSpec((8, d), lambda l, j: (0, 0)),
            pl.BlockSpec((None, d, tn), lambda l, j: (l, 0, j)),
            pl.BlockSpec((None, 1, tn), lambda l, j: (l, 0, j)),
        ],
        out_specs=pl.BlockSpec((None, 8, tn), lambda l, j: (l, 0, j)),
        out_shape=jax.ShapeDtypeStruct((depth, 8, n), F32),
        compiler_params=_params(blk, 2),
        name="ada",
    )(cvec, w_ada, b_ada.reshape(depth, 1, n))


def _mod_kernel(x_ref, g_ref, sh_ref, sc_ref, o_ref):
    x = x_ref[...]
    ms = jnp.mean(x * x, axis=-1, keepdims=True)
    y = x * lax.rsqrt(ms + RMS_EPS) * g_ref[...]
    o_ref[...] = (y * (1.0 + sc_ref[0]) + sh_ref[0]).astype(o_ref.dtype)


def _mod_call(x2, g, sh, sc, row_of_tile, tr):
    m, d = x2.shape
    vec = pl.BlockSpec((1, 1, d), lambda i: (row_of_tile(i), 0, 0))
    return pl.pallas_call(
        _mod_kernel,
        grid=(m // tr,),
        in_specs=[
            pl.BlockSpec((tr, d), lambda i: (i, 0)),
            pl.BlockSpec((1, d), lambda i: (0, 0)),
            vec, vec,
        ],
        out_specs=pl.BlockSpec((tr, d), lambda i: (i, 0)),
        out_shape=jax.ShapeDtypeStruct((m, d), BF16),
        compiler_params=_params(2 * tr * d * 6 + 3 * tr * d * 4, 1),
        name="modulate",
    )(x2, g.reshape(1, d), sh, sc)


def _rms_kernel(x_ref, g_ref, o_ref):
    x = x_ref[...]
    ms = jnp.mean(x * x, axis=-1, keepdims=True)
    o_ref[...] = x * lax.rsqrt(ms + RMS_EPS) * g_ref[...]


def _rms_call(x2, g, tr):
    m, d = x2.shape
    return pl.pallas_call(
        _rms_kernel,
        grid=(m // tr,),
        in_specs=[pl.BlockSpec((tr, d), lambda i: (i, 0)),
                  pl.BlockSpec((1, d), lambda i: (0, 0))],
        out_specs=pl.BlockSpec((tr, d), lambda i: (i, 0)),
        out_shape=jax.ShapeDtypeStruct((m, d), F32),
        compiler_params=_params(2 * tr * d * 8 + 2 * tr * d * 4, 1),
        name="final_norm",
    )(x2, g.reshape(1, d))


def _proj_kernel(a_ref, w_ref, b_ref, o_ref):
    acc = jnp.dot(a_ref[...], w_ref[...].astype(BF16), preferred_element_type=F32)
    o_ref[...] = (acc + b_ref[...]).astype(o_ref.dtype)


def _proj_call(a, w_stack, b_stack, layer, col0, n, tm, tn):
    m, k = a.shape
    depth, _, n_all = w_stack.shape
    jb = col0 // tn
    rows = tm * k * 2
    rest = 2 * (k * tn * 4 + tm * tn * 2 + tn * 4) + MATMUL_SPILL_BYTES
    row_buffers = 2 if 2 * rows + rest <= V7X_VMEM_LIMIT_BYTES else 1
    blk = row_buffers * rows + rest
    tmp = 0
    return pl.pallas_call(
        _proj_kernel,
        grid=(m // tm, n // tn),
        in_specs=[
            pl.BlockSpec((tm, k), lambda i, j: (i, 0), pipeline_mode=pl.Buffered(row_buffers)),
            pl.BlockSpec((None, k, tn), lambda i, j: (layer, 0, jb + j)),
            pl.BlockSpec((None, 1, tn), lambda i, j: (layer, 0, jb + j)),
        ],
        out_specs=pl.BlockSpec((tm, tn), lambda i, j: (i, j)),
        out_shape=jax.ShapeDtypeStruct((m, n), BF16),
        compiler_params=_params(blk + tmp, 2),
        name="in_proj",
    )(a, w_stack, b_stack.reshape(depth, 1, n_all))


def _dot_nt(a, b):
    return lax.dot_general(a, b, (((1,), (1,)), ((), ())), preferred_element_type=F32)


def _scaled_q(q_bf16):
    return (q_bf16.astype(F32) * (ATTN_SCALE * LOG2E)).astype(BF16)


def _with_ones(v):
    return jnp.concatenate([v, jnp.ones_like(v)], axis=1)


def _softmax_pv(scores, values, out_dtype):
    mx = functools.reduce(jnp.maximum, [jnp.max(s, axis=-1, keepdims=True) for s in scores])
    acc = functools.reduce(jnp.add, [jnp.dot(jnp.exp2(s - mx).astype(BF16), _with_ones(v), preferred_element_type=F32)
                                     for s, v in zip(scores, values)])
    return (acc[:, :HEAD_DIM] / acc[:, HEAD_DIM:]).astype(out_dtype)


def _band_start(g, n_rows):
    kr0 = jnp.clip(QROWS * g - NA_KH // 2, 0, n_rows - BAND_ROWS)
    return pl.multiple_of(kr0 * GRID_W, GRID_W)


def _na_scores(q_ref, k_ref, kc_ref, tab_ref, dst_ref, start):
    for hh in range(HEADS_PER_STEP):
        lanes = slice(hh * HEAD_DIM, (hh + 1) * HEAD_DIM)
        q = _scaled_q(q_ref[:, lanes])
        dst_ref[hh, :, :TK] = _dot_nt(q, k_ref[pl.ds(start, TK), lanes]) + tab_ref[hh]
        dst_ref[hh, :, TK:] = _dot_nt(q, kc_ref[:, lanes])


def _na_kernel(q_ref, k_ref, v_ref, kc_ref, vc_ref, za_ref, c2_ref, o_ref, tab_ref, sa_ref, sb_ref,
               *, n_groups, n_rows):
    step_id = pl.program_id(2)
    last = n_groups - 1
    g = jnp.minimum(step_id, last)

    @pl.when(jnp.logical_or(g <= 1, g == last))
    def _():
        interior = jnp.logical_and(g > 0, g < last)
        lo_a = jnp.where(g == last, BAND_ROWS - NA_KH, 0)
        lo_b = jnp.where(interior, 1, 0)
        off = jnp.where(g == 0, NA_KH - 1,
                        jnp.where(g == last, NA_KH - 1 - BAND_ROWS + QROWS, NA_KH - 1 - NA_KH // 2))
        qrow = jnp.right_shift(lax.broadcasted_iota(jnp.int32, (TQ, 1), 0), LOG2_GRID_W)
        lo = lo_a + lo_b * qrow
        jrow = jnp.right_shift(lax.broadcasted_iota(jnp.int32, (1, TK), 1), LOG2_GRID_W)
        row_ok = jnp.logical_and(jrow >= lo, jrow < lo + NA_KH)
        for hh in range(HEADS_PER_STEP):
            bias = jnp.concatenate(
                [jnp.concatenate([c2_ref[hh, 2 * m - i + off + QROWS] for m in range(BAND_ROWS // 2)], axis=1)
                 for i in range(QROWS)], axis=0)
            tab_ref[hh] = jnp.where(row_ok, bias * LOG2E, NEG_INF)

    @pl.when(step_id == 0)
    def _():
        _na_scores(q_ref, k_ref, kc_ref, tab_ref, sb_ref, _band_start(0, n_rows))

    kstart = _band_start(g, n_rows)
    vstart = _band_start(jnp.maximum(step_id - 1, 0), n_rows)

    def step(src_ref, dst_ref):
        _na_scores(q_ref, k_ref, kc_ref, tab_ref, dst_ref, kstart)
        for hh in range(HEADS_PER_STEP):
            lanes = slice(hh * HEAD_DIM, (hh + 1) * HEAD_DIM)
            y = _softmax_pv((src_ref[hh, :, :TK], src_ref[hh, :, TK:]),
                            (v_ref[pl.ds(vstart, TK), lanes], vc_ref[:, lanes]), o_ref.dtype)
            o_ref[:, lanes] = y * _silu_bf16(za_ref[:, lanes])

    @pl.when(step_id % 2 == 0)
    def _():
        step(sb_ref, sa_ref)

    @pl.when(step_id % 2 == 1)
    def _():
        step(sa_ref, sb_ref)


def _na_call(p, pc, c2, layer, batch, seq, ctx_len, kc_col, vc_col):
    n_rows = seq // GRID_W
    n_groups = n_rows // QROWS
    last = n_groups - 1
    wb = HEADS_PER_STEP * HEAD_DIM
    n_c2 = c2.shape[2]
    c2_bytes = HEADS_PER_STEP * n_c2 * GRID_W * 2 * GRID_W * 4
    tab_bytes = HEADS_PER_STEP * TQ * TK * 4
    score_bytes = HEADS_PER_STEP * TQ * (TK + ctx_len) * 4
    blk = 2 * (TQ * wb * 2 * 3 + 2 * seq * wb * 2 + 2 * ctx_len * wb * 2 + c2_bytes)
    kern = functools.partial(_na_kernel, n_groups=n_groups, n_rows=n_rows)
    return pl.pallas_call(
        kern,
        grid=(batch, N_HEADS // HEADS_PER_STEP, n_groups + 1),
        in_specs=[
            pl.BlockSpec((TQ, wb), lambda b, h, s: (b * n_groups + jnp.minimum(s, last), COL_Q // wb + h)),
            pl.BlockSpec((seq, wb), lambda b, h, s: (b, COL_K // wb + h)),
            pl.BlockSpec((seq, wb), lambda b, h, s: (b, COL_V // wb + h)),
            pl.BlockSpec((ctx_len, wb), lambda b, h, s: (b, kc_col // wb + h)),
            pl.BlockSpec((ctx_len, wb), lambda b, h, s: (b, vc_col // wb + h)),
            pl.BlockSpec((TQ, wb), lambda b, h, s: (b * n_groups + jnp.maximum(s - 1, 0), COL_ZA // wb + h)),
            pl.BlockSpec((None, HEADS_PER_STEP, n_c2, GRID_W, 2 * GRID_W), lambda b, h, s: (layer, h, 0, 0, 0)),
        ],
        out_specs=pl.BlockSpec((TQ, wb), lambda b, h, s: (b * n_groups + jnp.maximum(s - 1, 0), h)),
        out_shape=jax.ShapeDtypeStruct((batch * seq, D_ATTN), BF16),
        scratch_shapes=[pltpu.VMEM((HEADS_PER_STEP, TQ, TK), F32),
                        pltpu.VMEM((HEADS_PER_STEP, TQ, TK + ctx_len), F32),
                        pltpu.VMEM((HEADS_PER_STEP, TQ, TK + ctx_len), F32)],
        compiler_params=_params(blk + tab_bytes + 2 * score_bytes, 3),
        name="na_attn",
    )(p, p, p, pc, pc, p, c2)


def _ctx_attn_kernel(q_ref, k_ref, v_ref, za_ref, o_ref):
    s = _dot_nt(_scaled_q(q_ref[...]), k_ref[...])
    o_ref[...] = _softmax_pv((s,), (v_ref[...],), o_ref.dtype) * _silu_bf16(za_ref[...])


def _ctx_attn_call(pc, batch, ctx_len):
    hb = HEAD_DIM
    return pl.pallas_call(
        _ctx_attn_kernel,
        grid=(batch, N_HEADS),
        in_specs=[
            pl.BlockSpec((ctx_len, hb), lambda b, h: (b, COL_Q // hb + h)),
            pl.BlockSpec((ctx_len, hb), lambda b, h: (b, COL_K // hb + h)),
            pl.BlockSpec((ctx_len, hb), lambda b, h: (b, COL_V // hb + h)),
            pl.BlockSpec((ctx_len, hb), lambda b, h: (b, COL_ZA // hb + h)),
        ],
        out_specs=pl.BlockSpec((ctx_len, hb), lambda b, h: (b, h)),
        out_shape=jax.ShapeDtypeStruct((batch * ctx_len, D_ATTN), BF16),
        compiler_params=_params(8 * ctx_len * hb * 2 + 8 * ctx_len * ctx_len * 4, 2),
        name="ctx_attn",
    )(pc, pc, pc, pc)


def _bias_pair_table(rpb):
    qc = np.arange(GRID_W)[:, None]
    kc = np.arange(GRID_W)[None, :]
    ws = np.clip(qc - NA_KW // 2, 0, GRID_W - NA_KW)
    col_ok = (kc >= ws) & (kc < ws + NA_KW)
    dc = kc - qc + NA_KW - 1
    onehot = (dc[None] == np.arange(2 * NA_KW - 1)[:, None, None]) & col_ok[None]
    full = jnp.einsum("lhdx,xqk->lhdqk", rpb, jnp.asarray(onehot, F32), precision=lax.Precision.HIGHEST)
    full = full + jnp.asarray(np.where(col_ok, 0.0, NEG_INF), F32)
    n_e = BAND_ROWS + NA_KH - 1 + QROWS - 1
    extra = n_e - (2 * NA_KH - 1)
    edge = lambda lo: jnp.pad(full, ((0, 0), (0, 0), (lo, extra - lo), (0, 0), (0, 0)), mode="edge")
    return jnp.concatenate([edge(QROWS), edge(QROWS - 1)], axis=-1)


def _pool_gate_kernel(ucur_ref, uprev_ref, unext_ref, zp_ref, wp_ref, sp_ref, o_ref, *, tiles_per_seq, seq):
    t = ucur_ref.shape[0]
    cg = D_POOL_GROUP
    base = (pl.program_id(0) % tiles_per_seq) * t
    ucat = jnp.concatenate([uprev_ref[...], ucur_ref[...], unext_ref[...]], axis=0)
    tc = t + 2 * POOL_HALO
    row = lax.broadcasted_iota(jnp.int32, (t, tc), 0) + base
    col = lax.broadcasted_iota(jnp.int32, (t, tc), 1) + (base - POOL_HALO)
    tpos = lax.broadcasted_iota(jnp.int32, (t, 1), 0) + base
    for gi, w in enumerate(POOL_WINDOWS):
        cols = slice(gi * cg, (gi + 1) * cg)
        lo = jnp.maximum(row - w // 2, 0)
        hi = jnp.minimum(row - w // 2 + w - 1, seq - 1)
        band = jnp.where(jnp.logical_and(col >= lo, col <= hi), 1.0, 0.0).astype(BF16)
        wsum = jnp.dot(band, ucat[:, cols], preferred_element_type=F32)
        cnt = (jnp.minimum(tpos - w // 2 + w - 1, seq - 1) - jnp.maximum(tpos - w // 2, 0) + 1).astype(F32)
        pooled = wsum / cnt - ucur_ref[:, cols].astype(F32)
        y = jnp.dot(pooled.astype(BF16), wp_ref[gi].astype(BF16), preferred_element_type=F32) * sp_ref[:, cols]
        o_ref[:, cols] = y.astype(BF16) * _silu_bf16(zp_ref[:, cols])


def _pool_gate_call(p, w_pool, s_pool_l, layer, seq, t):
    m = p.shape[0]
    tiles_per_seq = seq // t
    hpt = t // POOL_HALO
    n_halo_blocks = m // POOL_HALO
    wp_shape = w_pool.shape[1:]
    blk = 2 * (3 * t * D_POOL * 2 + 2 * POOL_HALO * D_POOL * 2 + int(np.prod(wp_shape)) * 4 + D_POOL * 4)
    tmp = 4 * t * (t + 2 * POOL_HALO) * 4 + 8 * t * D_POOL_GROUP * 4
    kern = functools.partial(_pool_gate_kernel, tiles_per_seq=tiles_per_seq, seq=seq)
    return pl.pallas_call(
        kern,
        grid=(m // t,),
        in_specs=[
            pl.BlockSpec((t, D_POOL), lambda i: (i, COL_U // D_POOL)),
            pl.BlockSpec((POOL_HALO, D_POOL), lambda i: (jnp.maximum(i * hpt - 1, 0), COL_U // D_POOL)),
            pl.BlockSpec((POOL_HALO, D_POOL),
                         lambda i: (jnp.minimum((i + 1) * hpt, n_halo_blocks - 1), COL_U // D_POOL)),
            pl.BlockSpec((t, D_POOL), lambda i: (i, COL_ZP // D_POOL)),
            pl.BlockSpec((None,) + wp_shape, lambda i: (layer, 0, 0, 0)),
            pl.BlockSpec((1, D_POOL), lambda i: (0, 0)),
        ],
        out_specs=pl.BlockSpec((t, D_POOL), lambda i: (i, 0)),
        out_shape=jax.ShapeDtypeStruct((m, D_POOL), BF16),
        compiler_params=_params(blk + tmp, 1),
        name="pool_gate",
    )(p, p, p, p, w_pool, s_pool_l.reshape(1, D_POOL))


def _sigmoid_bf16(z):
    return 0.5 + 0.5 * jnp.tanh(z * 0.5)


def _merge_kernel(ap_ref, aa_ref, wp_ref, wa_ref, gp_ref, ga_ref, o_ref):
    br_p = jnp.dot(ap_ref[...], wp_ref[...].astype(BF16), preferred_element_type=F32)
    br_a = jnp.dot(aa_ref[...], wa_ref[...].astype(BF16), preferred_element_type=F32)
    gp = _sigmoid_bf16(gp_ref[...]).astype(F32)
    ga = _sigmoid_bf16(ga_ref[...]).astype(F32)
    o_ref[...] = (gp * br_p + ga * br_a).astype(o_ref.dtype)


def _merge_call(a_pool, a_attn, w_br_pool, w_br_attn, p, layer, tm, tn):
    m = a_pool.shape[0]
    n = D_MODEL
    blk = 2 * (tm * D_MODEL * 2 + 2 * D_POOL * tn * 4 + 3 * tm * tn * 2)
    tmp = 2 * tm * tn * 4
    return pl.pallas_call(
        _merge_kernel,
        grid=(m // tm, n // tn),
        in_specs=[
            pl.BlockSpec((tm, D_POOL), lambda i, j: (i, 0)),
            pl.BlockSpec((tm, D_ATTN), lambda i, j: (i, 0)),
            pl.BlockSpec((None, D_POOL, tn), lambda i, j: (layer, 0, j)),
            pl.BlockSpec((None, D_ATTN, tn), lambda i, j: (layer, 0, j)),
            pl.BlockSpec((tm, tn), lambda i, j: (i, COL_GP // tn + j)),
            pl.BlockSpec((tm, tn), lambda i, j: (i, COL_GA // tn + j)),
        ],
        out_specs=pl.BlockSpec((tm, tn), lambda i, j: (i, j)),
        out_shape=jax.ShapeDtypeStruct((m, n), BF16),
        compiler_params=_params(blk + tmp, 2),
        name="merge",
    )(a_pool, a_attn, w_br_pool, w_br_attn, p, p)


def _out_kernel(m_ref, w_ref, x_ref, gt_ref, o_ref):
    acc = jnp.dot(m_ref[...], w_ref[...].astype(BF16), preferred_element_type=F32)
    o_ref[...] = x_ref[...] + gt_ref[0] * acc


def _out_call(mix, w_out, x2, gt, layer, row_of_tile, tm, tn):
    m, k = mix.shape
    n = w_out.shape[2]
    blk = tm * k * 2 + 2 * (k * tn * 4 + 2 * tm * tn * 4 + tn * 4)
    tmp = tm * tn * 4
    return pl.pallas_call(
        _out_kernel,
        grid=(m // tm, n // tn),
        in_specs=[
            pl.BlockSpec((tm, k), lambda i, j: (i, 0), pipeline_mode=pl.Buffered(1)),
            pl.BlockSpec((None, k, tn), lambda i, j: (layer, 0, j)),
            pl.BlockSpec((tm, tn), lambda i, j: (i, j)),
            pl.BlockSpec((1, 1, tn), lambda i, j: (row_of_tile(i), 0, j)),
        ],
        out_specs=pl.BlockSpec((tm, tn), lambda i, j: (i, j)),
        out_shape=jax.ShapeDtypeStruct((m, n), F32),
        compiler_params=_params(blk + tmp, 2),
        name="out_proj",
    )(mix, w_out, x2, gt)


def kernel(x, c, ctx, c_ctx, norm_g, w_ada, b_ada, w_in, b_in, w_pool, s_pool, rpb,
           w_br_pool, w_br_attn, w_out, final_g):
    batch, seq, d = x.shape
    ctx_len = ctx.shape[1]
    depth = w_in.shape[0]
    ctx_row = batch

    x_lat = x.reshape(batch * seq, d)
    x_ctx = ctx.reshape(batch * ctx_len, d)
    cvec = jnp.zeros((8, d), F32).at[:batch].set(c).at[ctx_row].set(c_ctx)
    ada = _ada_call(cvec, w_ada, b_ada)
    c2 = _bias_pair_table(rpb)

    tm_in, tn_in = 2048, 512
    tm_out, tn_out = 2048, 512
    tm_mrg, tn_mrg = 1024, 512
    tm_ctx = batch * ctx_len
    t_row = 256
    t_norm = 512
    lat_row = lambda tile_rows: (lambda i: i // (seq // tile_rows))
    ctx_row_fn = lambda i: ctx_row

    for l in range(depth):
        last = l == depth - 1
        mod = ada[l].reshape(8, 3, 1, d)
        sh, sc, gt = mod[:, 0], mod[:, 1], mod[:, 2]

        h_lat = _mod_call(x_lat, norm_g[l], sh, sc, lat_row(t_norm), t_norm)
        h_ctx = _mod_call(x_ctx, norm_g[l], sh, sc, ctx_row_fn, t_norm)
        p_lat = _proj_call(h_lat, w_in, b_in, l, 0, D_IN, tm_in, tn_in)
        if last:
            p_ctx = _proj_call(h_ctx, w_in, b_in, l, COL_K, 2 * D_ATTN, tm_ctx, tn_in)
            kc_col, vc_col = 0, D_ATTN
        else:
            p_ctx = _proj_call(h_ctx, w_in, b_in, l, 0, D_IN, tm_ctx, tn_in)
            kc_col, vc_col = COL_K, COL_V

        a_attn = _na_call(p_lat, p_ctx, c2, l, batch, seq, ctx_len, kc_col, vc_col)
        a_pool = _pool_gate_call(p_lat, w_pool, s_pool[l], l, seq, t_row)
        mix = _merge_call(a_pool, a_attn, w_br_pool, w_br_attn, p_lat, l, tm_mrg, tn_mrg)
        x_lat_new = _out_call(mix, w_out, x_lat, gt, l, lat_row(tm_out), tm_out, tn_out)

        if not last:
            a_attn_c = _ctx_attn_call(p_ctx, batch, ctx_len)
            a_pool_c = _pool_gate_call(p_ctx, w_pool, s_pool[l], l, ctx_len, t_row)
            mix_c = _merge_call(a_pool_c, a_attn_c, w_br_pool, w_br_attn, p_ctx, l, tm_ctx, tn_mrg)
            x_ctx = _out_call(mix_c, w_out, x_ctx, gt, l, ctx_row_fn, tm_ctx, tn_out)
        x_lat = x_lat_new

    return _rms_call(x_lat, final_g, t_norm).reshape(batch, seq, d)
```

```python
import functools

import numpy as np
import jax
import jax.numpy as jnp
from jax import lax
from jax.experimental import pallas as pl
from jax.experimental.pallas import tpu as pltpu

F32 = jnp.float32
BF16 = jnp.bfloat16

D_MODEL = 4096
GRID_W = 64
LOG2_GRID_W = 6
D_POOL = D_MODEL // 2
POOL_WINDOWS = (2, 4, 8, 16)
D_POOL_GROUP = D_POOL // len(POOL_WINDOWS)
HEAD_DIM = 128
D_ATTN = D_MODEL // 2
N_HEADS = D_ATTN // HEAD_DIM
NA_KH = 8
NA_KW = 16
D_IN = 2 * D_POOL + 4 * D_ATTN + 2 * D_MODEL
RMS_EPS = 1e-6
NEG_INF = -1e30
ATTN_SCALE = HEAD_DIM ** -0.5
LOG2E = 1.4426950408889634

COL_U = 0
COL_ZP = D_POOL
COL_Q = 2 * D_POOL
COL_K = COL_Q + D_ATTN
COL_V = COL_K + D_ATTN
COL_ZA = COL_V + D_ATTN
COL_GP = COL_ZA + D_ATTN
COL_GA = COL_GP + D_MODEL

V7X_VMEM_LIMIT_BYTES = 60000 * 1024
COMPILER_SCRATCH_BYTES = 16 << 20
MATMUL_SPILL_BYTES = 4 << 20

QROWS = 4
BAND_ROWS = 12
TQ = QROWS * GRID_W
TK = BAND_ROWS * GRID_W
HEADS_PER_STEP = 4
POOL_HALO = 64
POOL_SUBTILE = 128


def _params(block_bytes, n_axes):
    return pltpu.CompilerParams(
        dimension_semantics=("arbitrary",) * n_axes,
        vmem_limit_bytes=int(min(V7X_VMEM_LIMIT_BYTES, block_bytes + COMPILER_SCRATCH_BYTES)),
    )


def _sigmoid(x):
    return 1.0 / (1.0 + jnp.exp(-x))


def _silu_bf16(z):
    hz = z * 0.5
    return hz + hz * jnp.tanh(hz)


def _split_bf16(v):
    hi = v.astype(BF16)
    lo = (v - hi.astype(F32)).astype(BF16)
    return hi, lo


def _ada_kernel(c_ref, w_ref, b_ref, o_ref):
    cv = c_ref[...]
    s_hi, s_lo = _split_bf16(cv * _sigmoid(cv))
    w_hi, w_lo = _split_bf16(w_ref[...])
    rows = s_hi.shape[0]
    r_hi = jnp.dot(jnp.concatenate([s_hi, s_lo], axis=0), w_hi, preferred_element_type=F32)
    r_lo = jnp.dot(s_hi, w_lo, preferred_element_type=F32)
    o_ref[...] = r_hi[:rows] + r_hi[rows:] + r_lo + b_ref[...]


def _ada_call(cvec, w_ada, b_ada):
    depth, d, n = w_ada.shape
    tn = 512
    blk = 2 * (d * tn * 4) + 2 * 8 * d * 4 + 4 * 8 * tn * 4
    return pl.pallas_call(
        _ada_kernel,
        grid=(depth, n // tn),
        in_specs=[
            pl.BlockSpec((8, d), lambda l, j: (0, 0)),
            pl.BlockSpec((None, d, tn), lambda l, j: (l, 0, j)),
            pl.BlockSpec((None, 1, tn), lambda l, j: (l, 0, j)),
        ],
        out_specs=pl.BlockSpec((None, 8, tn), lambda l, j: (l, 0, j)),
        out_shape=jax.ShapeDtypeStruct((depth, 8, n), F32),
        compiler_params=_params(blk, 2),
        name="ada",
    )(cvec, w_ada, b_ada.reshape(depth, 1, n))


def _mod_kernel(x_ref, g_ref, sh_ref, sc_ref, o_ref):
    x = x_ref[...]
    ms = jnp.mean(x * x, axis=-1, keepdims=True)
    y = x * lax.rsqrt(ms + RMS_EPS) * g_ref[...]
    o_ref[...] = (y * (1.0 + sc_ref[0]) + sh_ref[0]).astype(o_ref.dtype)


def _mod_call(x2, g, sh, sc, row_of_tile, tr):
    m, d = x2.shape
    vec = pl.BlockSpec((1, 1, d), lambda i: (row_of_tile(i), 0, 0))
    return pl.pallas_call(
        _mod_kernel,
        grid=(m // tr,),
        in_specs=[
            pl.BlockSpec((tr, d), lambda i: (i, 0)),
            pl.BlockSpec((1, d), lambda i: (0, 0)),
            vec, vec,
        ],
        out_specs=pl.BlockSpec((tr, d), lambda i: (i, 0)),
        out_shape=jax.ShapeDtypeStruct((m, d), BF16),
        compiler_params=_params(2 * tr * d * 6 + 3 * tr * d * 4, 1),
        name="modulate",
    )(x2, g.reshape(1, d), sh, sc)


def _rms_kernel(x_ref, g_ref, o_ref):
    x = x_ref[...]
    ms = jnp.mean(x * x, axis=-1, keepdims=True)
    o_ref[...] = x * lax.rsqrt(ms + RMS_EPS) * g_ref[...]


def _rms_call(x2, g, tr):
    m, d = x2.shape
    return pl.pallas_call(
        _rms_kernel,
        grid=(m // tr,),
        in_specs=[pl.BlockSpec((tr, d), lambda i: (i, 0)),
                  pl.BlockSpec((1, d), lambda i: (0, 0))],
        out_specs=pl.BlockSpec((tr, d), lambda i: (i, 0)),
        out_shape=jax.ShapeDtypeStruct((m, d), F32),
        compiler_params=_params(2 * tr * d * 8 + 2 * tr * d * 4, 1),
        name="final_norm",
    )(x2, g.reshape(1, d))


def _proj_kernel(a_ref, w_ref, b_ref, o_ref):
    acc = jnp.dot(a_ref[...], w_ref[...].astype(BF16), preferred_element_type=F32)
    o_ref[...] = (acc + b_ref[...]).astype(o_ref.dtype)


def _proj_call(a, w_stack, b_stack, layer, col0, n, tm, tn):
    m, k = a.shape
    depth, _, n_all = w_stack.shape
    jb = col0 // tn
    rows = tm * k * 2
    rest = 2 * (k * tn * 4 + tm * tn * 2 + tn * 4) + MATMUL_SPILL_BYTES
    row_buffers = 2 if 2 * rows + rest <= V7X_VMEM_LIMIT_BYTES else 1
    blk = row_buffers * rows + rest
    tmp = 0
    return pl.pallas_call(
        _proj_kernel,
        grid=(m // tm, n // tn),
        in_specs=[
            pl.BlockSpec((tm, k), lambda i, j: (i, 0), pipeline_mode=pl.Buffered(row_buffers)),
            pl.BlockSpec((None, k, tn), lambda i, j: (layer, 0, jb + j)),
            pl.BlockSpec((None, 1, tn), lambda i, j: (layer, 0, jb + j)),
        ],
        out_specs=pl.BlockSpec((tm, tn), lambda i, j: (i, j)),
        out_shape=jax.ShapeDtypeStruct((m, n), BF16),
        compiler_params=_params(blk + tmp, 2),
        name="in_proj",
    )(a, w_stack, b_stack.reshape(depth, 1, n_all))


def _dot_nt(a, b):
    return lax.dot_general(a, b, (((1,), (1,)), ((), ())), preferred_element_type=F32)


def _scaled_q(q_bf16):
    return (q_bf16.astype(F32) * (ATTN_SCALE * LOG2E)).astype(BF16)


def _with_ones(v):
    return jnp.concatenate([v, jnp.ones_like(v)], axis=1)


def _softmax_pv(scores, values, out_dtype):
    mx = functools.reduce(jnp.maximum, [jnp.max(s, axis=-1, keepdims=True) for s in scores])
    acc = functools.reduce(jnp.add, [jnp.dot(jnp.exp2(s - mx).astype(BF16), _with_ones(v), preferred_element_type=F32)
                                     for s, v in zip(scores, values)])
    return (acc[:, :HEAD_DIM] / acc[:, HEAD_DIM:]).astype(out_dtype)


def _band_start(g, n_rows):
    kr0 = jnp.clip(QROWS * g - NA_KH // 2, 0, n_rows - BAND_ROWS)
    return pl.multiple_of(kr0 * GRID_W, GRID_W)


def _na_scores(q_ref, k_ref, kc_ref, tab_ref, dst_ref, start):
    for hh in range(HEADS_PER_STEP):
        lanes = slice(hh * HEAD_DIM, (hh + 1) * HEAD_DIM)
        q = _scaled_q(q_ref[:, lanes])
        dst_ref[hh, :, :TK] = _dot_nt(q, k_ref[pl.ds(start, TK), lanes]) + tab_ref[hh]
        dst_ref[hh, :, TK:] = _dot_nt(q, kc_ref[:, lanes])


def _na_kernel(q_ref, k_ref, v_ref, kc_ref, vc_ref, za_ref, c2_ref, o_ref, tab_ref, sa_ref, sb_ref,
               *, n_groups, n_rows):
    step_id = pl.program_id(2)
    last = n_groups - 1
    g = jnp.minimum(step_id, last)

    @pl.when(jnp.logical_or(g <= 1, g == last))
    def _():
        interior = jnp.logical_and(g > 0, g < last)
        lo_a = jnp.where(g == last, BAND_ROWS - NA_KH, 0)
        lo_b = jnp.where(interior, 1, 0)
        off = jnp.where(g == 0, NA_KH - 1,
                        jnp.where(g == last, NA_KH - 1 - BAND_ROWS + QROWS, NA_KH - 1 - NA_KH // 2))
        qrow = jnp.right_shift(lax.broadcasted_iota(jnp.int32, (TQ, 1), 0), LOG2_GRID_W)
        lo = lo_a + lo_b * qrow
        jrow = jnp.right_shift(lax.broadcasted_iota(jnp.int32, (1, TK), 1), LOG2_GRID_W)
        row_ok = jnp.logical_and(jrow >= lo, jrow < lo + NA_KH)
        for hh in range(HEADS_PER_STEP):
            bias = jnp.concatenate(
                [jnp.concatenate([c2_ref[hh, 2 * m - i + off + QROWS] for m in range(BAND_ROWS // 2)], axis=1)
                 for i in range(QROWS)], axis=0)
            tab_ref[hh] = jnp.where(row_ok, bias * LOG2E, NEG_INF)

    @pl.when(step_id == 0)
    def _():
        _na_scores(q_ref, k_ref, kc_ref, tab_ref, sb_ref, _band_start(0, n_rows))

    kstart = _band_start(g, n_rows)
    vstart = _band_start(jnp.maximum(step_id - 1, 0), n_rows)

    def step(src_ref, dst_ref):
        _na_scores(q_ref, k_ref, kc_ref, tab_ref, dst_ref, kstart)
        for hh in range(HEADS_PER_STEP):
            lanes = slice(hh * HEAD_DIM, (hh + 1) * HEAD_DIM)
            y = _softmax_pv((src_ref[hh, :, :TK], src_ref[hh, :, TK:]),
                            (v_ref[pl.ds(vstart, TK), lanes], vc_ref[:, lanes]), o_ref.dtype)
            o_ref[:, lanes] = y * _silu_bf16(za_ref[:, lanes])

    @pl.when(step_id % 2 == 0)
    def _():
        step(sb_ref, sa_ref)

    @pl.when(step_id % 2 == 1)
    def _():
        step(sa_ref, sb_ref)


def _na_call(p, pc, c2, layer, batch, seq, ctx_len, kc_col, vc_col):
    n_rows = seq // GRID_W
    n_groups = n_rows // QROWS
    last = n_groups - 1
    wb = HEADS_PER_STEP * HEAD_DIM
    n_c2 = c2.shape[2]
    c2_bytes = HEADS_PER_STEP * n_c2 * GRID_W * 2 * GRID_W * 4
    tab_bytes = HEADS_PER_STEP * TQ * TK * 4
    score_bytes = HEADS_PER_STEP * TQ * (TK + ctx_len) * 4
    blk = 2 * (TQ * wb * 2 * 3 + 2 * seq * wb * 2 + 2 * ctx_len * wb * 2 + c2_bytes)
    kern = functools.partial(_na_kernel, n_groups=n_groups, n_rows=n_rows)
    return pl.pallas_call(
        kern,
        grid=(batch, N_HEADS // HEADS_PER_STEP, n_groups + 1),
        in_specs=[
            pl.BlockSpec((TQ, wb), lambda b, h, s: (b * n_groups + jnp.minimum(s, last), COL_Q // wb + h)),
            pl.BlockSpec((seq, wb), lambda b, h, s: (b, COL_K // wb + h)),
            pl.BlockSpec((seq, wb), lambda b, h, s: (b, COL_V // wb + h)),
            pl.BlockSpec((ctx_len, wb), lambda b, h, s: (b, kc_col // wb + h)),
            pl.BlockSpec((ctx_len, wb), lambda b, h, s: (b, vc_col // wb + h)),
            pl.BlockSpec((TQ, wb), lambda b, h, s: (b * n_groups + jnp.maximum(s - 1, 0), COL_ZA // wb + h)),
            pl.BlockSpec((None, HEADS_PER_STEP, n_c2, GRID_W, 2 * GRID_W), lambda b, h, s: (layer, h, 0, 0, 0)),
        ],
        out_specs=pl.BlockSpec((TQ, wb), lambda b, h, s: (b * n_groups + jnp.maximum(s - 1, 0), h)),
        out_shape=jax.ShapeDtypeStruct((batch * seq, D_ATTN), BF16),
        scratch_shapes=[pltpu.VMEM((HEADS_PER_STEP, TQ, TK), F32),
                        pltpu.VMEM((HEADS_PER_STEP, TQ, TK + ctx_len), F32),
                        pltpu.VMEM((HEADS_PER_STEP, TQ, TK + ctx_len), F32)],
        compiler_params=_params(blk + tab_bytes + 2 * score_bytes, 3),
        name="na_attn",
    )(p, p, p, pc, pc, p, c2)


def _ctx_attn_kernel(q_ref, k_ref, v_ref, za_ref, o_ref):
    for hh in range(HEADS_PER_STEP):
        lanes = slice(hh * HEAD_DIM, (hh + 1) * HEAD_DIM)
        s = _dot_nt(_scaled_q(q_ref[:, lanes]), k_ref[:, lanes])
        o_ref[:, lanes] = _softmax_pv((s,), (v_ref[:, lanes],), o_ref.dtype) * _silu_bf16(za_ref[:, lanes])


def _ctx_attn_call(pc, batch, ctx_len):
    hb = HEADS_PER_STEP * HEAD_DIM
    return pl.pallas_call(
        _ctx_attn_kernel,
        grid=(batch, N_HEADS // HEADS_PER_STEP),
        in_specs=[
            pl.BlockSpec((ctx_len, hb), lambda b, h: (b, COL_Q // hb + h)),
            pl.BlockSpec((ctx_len, hb), lambda b, h: (b, COL_K // hb + h)),
            pl.BlockSpec((ctx_len, hb), lambda b, h: (b, COL_V // hb + h)),
            pl.BlockSpec((ctx_len, hb), lambda b, h: (b, COL_ZA // hb + h)),
        ],
        out_specs=pl.BlockSpec((ctx_len, hb), lambda b, h: (b, h)),
        out_shape=jax.ShapeDtypeStruct((batch * ctx_len, D_ATTN), BF16),
        compiler_params=_params(8 * ctx_len * hb * 2 + 8 * ctx_len * ctx_len * 4, 2),
        name="ctx_attn",
    )(pc, pc, pc, pc)


def _bias_pair_table(rpb):
    qc = np.arange(GRID_W)[:, None]
    kc = np.arange(GRID_W)[None, :]
    ws = np.clip(qc - NA_KW // 2, 0, GRID_W - NA_KW)
    col_ok = (kc >= ws) & (kc < ws + NA_KW)
    dc = kc - qc + NA_KW - 1
    onehot = (dc[None] == np.arange(2 * NA_KW - 1)[:, None, None]) & col_ok[None]
    full = jnp.einsum("lhdx,xqk->lhdqk", rpb, jnp.asarray(onehot, F32), precision=lax.Precision.HIGHEST)
    full = full + jnp.asarray(np.where(col_ok, 0.0, NEG_INF), F32)
    n_e = BAND_ROWS + NA_KH - 1 + QROWS - 1
    extra = n_e - (2 * NA_KH - 1)
    edge = lambda lo: jnp.concatenate([full[:, :, :1]] * lo + [full] + [full[:, :, -1:]] * (extra - lo), axis=2)
    return jnp.concatenate([edge(QROWS), edge(QROWS - 1)], axis=-1)


def _pool_bands():
    r = np.arange(POOL_SUBTILE)[:, None]
    c = np.arange(POOL_SUBTILE + 2 * POOL_HALO)[None, :] - POOL_HALO
    return np.stack([(c >= r - w // 2) & (c <= r - w // 2 + w - 1) for w in POOL_WINDOWS]).astype(np.float32)


def _pool_gate_kernel(ucur_ref, uprev_ref, unext_ref, zp_ref, band_ref, wp_ref, sp_ref, o_ref,
                      *, tiles_per_seq, seq):
    t = ucur_ref.shape[0]
    cg = D_POOL_GROUP
    ts = pl.program_id(0) % tiles_per_seq
    base = ts * t
    uprev = jnp.where(ts > 0, uprev_ref[...], jnp.zeros_like(uprev_ref))
    unext = jnp.where(ts < tiles_per_seq - 1, unext_ref[...], jnp.zeros_like(unext_ref))
    ucat = jnp.concatenate([uprev, ucur_ref[...], unext], axis=0)
    tpos = lax.broadcasted_iota(jnp.int32, (t, 1), 0) + base
    for gi, w in enumerate(POOL_WINDOWS):
        cols = slice(gi * cg, (gi + 1) * cg)
        wsum = jnp.concatenate(
            [jnp.dot(band_ref[gi], ucat[r0:r0 + POOL_SUBTILE + 2 * POOL_HALO, cols], preferred_element_type=F32)
             for r0 in range(0, t, POOL_SUBTILE)], axis=0)
        cnt = (jnp.minimum(tpos - w // 2 + w - 1, seq - 1) - jnp.maximum(tpos - w // 2, 0) + 1).astype(F32)
        pooled = wsum / cnt - ucur_ref[:, cols].astype(F32)
        y = jnp.dot(pooled.astype(BF16), wp_ref[gi].astype(BF16), preferred_element_type=F32) * sp_ref[:, cols]
        o_ref[:, cols] = y.astype(BF16) * _silu_bf16(zp_ref[:, cols])


def _pool_gate_call(p, w_pool, s_pool_l, layer, seq, t):
    m = p.shape[0]
    tiles_per_seq = seq // t
    hpt = t // POOL_HALO
    n_halo_blocks = m // POOL_HALO
    wp_shape = w_pool.shape[1:]
    bands = jnp.asarray(_pool_bands(), BF16)
    blk = 2 * (3 * t * D_POOL * 2 + 2 * POOL_HALO * D_POOL * 2 + int(np.prod(wp_shape)) * 4 + D_POOL * 4
               + bands.size * 2)
    tmp = 8 * t * D_POOL_GROUP * 4
    kern = functools.partial(_pool_gate_kernel, tiles_per_seq=tiles_per_seq, seq=seq)
    return pl.pallas_call(
        kern,
        grid=(m // t,),
        in_specs=[
            pl.BlockSpec((t, D_POOL), lambda i: (i, COL_U // D_POOL)),
            pl.BlockSpec((POOL_HALO, D_POOL), lambda i: (jnp.maximum(i * hpt - 1, 0), COL_U // D_POOL)),
            pl.BlockSpec((POOL_HALO, D_POOL),
                         lambda i: (jnp.minimum((i + 1) * hpt, n_halo_blocks - 1), COL_U // D_POOL)),
            pl.BlockSpec((t, D_POOL), lambda i: (i, COL_ZP // D_POOL)),
            pl.BlockSpec(bands.shape, lambda i: (0, 0, 0)),
            pl.BlockSpec((None,) + wp_shape, lambda i: (layer, 0, 0, 0)),
            pl.BlockSpec((1, D_POOL), lambda i: (0, 0)),
        ],
        out_specs=pl.BlockSpec((t, D_POOL), lambda i: (i, 0)),
        out_shape=jax.ShapeDtypeStruct((m, D_POOL), BF16),
        compiler_params=_params(blk + tmp, 1),
        name="pool_gate",
    )(p, p, p, p, bands, w_pool, s_pool_l.reshape(1, D_POOL))


def _sigmoid_bf16(z):
    return 0.5 + 0.5 * jnp.tanh(z * 0.5)


def _merge_kernel(ap_ref, aa_ref, wp_ref, wa_ref, gp_ref, ga_ref, o_ref):
    br_p = jnp.dot(ap_ref[...], wp_ref[...].astype(BF16), preferred_element_type=F32)
    br_a = jnp.dot(aa_ref[...], wa_ref[...].astype(BF16), preferred_element_type=F32)
    gp = _sigmoid_bf16(gp_ref[...]).astype(F32)
    ga = _sigmoid_bf16(ga_ref[...]).astype(F32)
    o_ref[...] = (gp * br_p + ga * br_a).astype(o_ref.dtype)


def _merge_call(a_pool, a_attn, w_br_pool, w_br_attn, p, layer, tm, tn):
    m = a_pool.shape[0]
    n = D_MODEL
    blk = 2 * (tm * D_MODEL * 2 + 2 * D_POOL * tn * 4 + 3 * tm * tn * 2)
    tmp = 2 * tm * tn * 4
    return pl.pallas_call(
        _merge_kernel,
        grid=(m // tm, n // tn),
        in_specs=[
            pl.BlockSpec((tm, D_POOL), lambda i, j: (i, 0)),
            pl.BlockSpec((tm, D_ATTN), lambda i, j: (i, 0)),
            pl.BlockSpec((None, D_POOL, tn), lambda i, j: (layer, 0, j)),
            pl.BlockSpec((None, D_ATTN, tn), lambda i, j: (layer, 0, j)),
            pl.BlockSpec((tm, tn), lambda i, j: (i, COL_GP // tn + j)),
            pl.BlockSpec((tm, tn), lambda i, j: (i, COL_GA // tn + j)),
        ],
        out_specs=pl.BlockSpec((tm, tn), lambda i, j: (i, j)),
        out_shape=jax.ShapeDtypeStruct((m, n), BF16),
        compiler_params=_params(blk + tmp, 2),
        name="merge",
    )(a_pool, a_attn, w_br_pool, w_br_attn, p, p)


def _out_kernel(m_ref, w_ref, x_ref, gt_ref, o_ref):
    acc = jnp.dot(m_ref[...], w_ref[...].astype(BF16), preferred_element_type=F32)
    o_ref[...] = x_ref[...] + gt_ref[0] * acc


def _out_call(mix, w_out, x2, gt, layer, row_of_tile, tm, tn):
    m, k = mix.shape
    n = w_out.shape[2]
    blk = tm * k * 2 + 2 * (k * tn * 4 + 2 * tm * tn * 4 + tn * 4)
    tmp = tm * tn * 4
    return pl.pallas_call(
        _out_kernel,
        grid=(m // tm, n // tn),
        in_specs=[
            pl.BlockSpec((tm, k), lambda i, j: (i, 0), pipeline_mode=pl.Buffered(1)),
            pl.BlockSpec((None, k, tn), lambda i, j: (layer, 0, j)),
            pl.BlockSpec((tm, tn), lambda i, j: (i, j)),
            pl.BlockSpec((1, 1, tn), lambda i, j: (row_of_tile(i), 0, j)),
        ],
        out_specs=pl.BlockSpec((tm, tn), lambda i, j: (i, j)),
        out_shape=jax.ShapeDtypeStruct((m, n), F32),
        compiler_params=_params(blk + tmp, 2),
        name="out_proj",
    )(mix, w_out, x2, gt)


def kernel(x, c, ctx, c_ctx, norm_g, w_ada, b_ada, w_in, b_in, w_pool, s_pool, rpb,
           w_br_pool, w_br_attn, w_out, final_g):
    batch, seq, d = x.shape
    ctx_len = ctx.shape[1]
    depth = w_in.shape[0]
    ctx_row = batch

    x_lat = x.reshape(batch * seq, d)
    x_ctx = ctx.reshape(batch * ctx_len, d)
    cvec = jnp.zeros((8, d), F32).at[:batch].set(c).at[ctx_row].set(c_ctx)
    ada = _ada_call(cvec, w_ada, b_ada)
    c2 = _bias_pair_table(rpb)

    tm_in, tn_in = 2048, 512
    tm_out, tn_out = 2048, 512
    tm_mrg, tn_mrg = 1024, 512
    tm_ctx = batch * ctx_len
    tn_ctx = 1024
    t_row = 256
    t_norm = 512
    lat_row = lambda tile_rows: (lambda i: i // (seq // tile_rows))
    ctx_row_fn = lambda i: ctx_row

    for l in range(depth):
        last = l == depth - 1
        mod = ada[l].reshape(8, 3, 1, d)
        sh, sc, gt = mod[:, 0], mod[:, 1], mod[:, 2]

        h_lat = _mod_call(x_lat, norm_g[l], sh, sc, lat_row(t_norm), t_norm)
        h_ctx = _mod_call(x_ctx, norm_g[l], sh, sc, ctx_row_fn, t_norm)
        p_lat = _proj_call(h_lat, w_in, b_in, l, 0, D_IN, tm_in, tn_in)
        if last:
            p_ctx = _proj_call(h_ctx, w_in, b_in, l, COL_K, 2 * D_ATTN, tm_ctx, tn_ctx)
            kc_col, vc_col = 0, D_ATTN
        else:
            p_ctx = _proj_call(h_ctx, w_in, b_in, l, 0, D_IN, tm_ctx, tn_ctx)
            kc_col, vc_col = COL_K, COL_V

        a_attn = _na_call(p_lat, p_ctx, c2, l, batch, seq, ctx_len, kc_col, vc_col)
        a_pool = _pool_gate_call(p_lat, w_pool, s_pool[l], l, seq, t_row)
        mix = _merge_call(a_pool, a_attn, w_br_pool, w_br_attn, p_lat, l, tm_mrg, tn_mrg)
        x_lat_new = _out_call(mix, w_out, x_lat, gt, l, lat_row(tm_out), tm_out, tn_out)

        if not last:
            a_attn_c = _ctx_attn_call(p_ctx, batch, ctx_len)
            a_pool_c = _pool_gate_call(p_ctx, w_pool, s_pool[l], l, ctx_len, t_row)
            mix_c = _merge_call(a_pool_c, a_attn_c, w_br_pool, w_br_attn, p_ctx, l, tm_ctx, tn_ctx)
            x_ctx = _out_call(mix_c, w_out, x_ctx, gt, l, ctx_row_fn, tm_ctx, tn_ctx)
        x_lat = x_lat_new

    return _rms_call(x_lat, final_g, t_norm).reshape(batch, seq, d)
```

```python
import functools

import numpy as np
import jax
import jax.numpy as jnp
from jax import lax
from jax.experimental import pallas as pl
from jax.experimental.pallas import tpu as pltpu

F32 = jnp.float32
BF16 = jnp.bfloat16

D_MODEL = 4096
GRID_W = 64
LOG2_GRID_W = 6
D_POOL = D_MODEL // 2
POOL_WINDOWS = (2, 4, 8, 16)
D_POOL_GROUP = D_POOL // len(POOL_WINDOWS)
HEAD_DIM = 128
D_ATTN = D_MODEL // 2
N_HEADS = D_ATTN // HEAD_DIM
NA_KH = 8
NA_KW = 16
D_IN = 2 * D_POOL + 4 * D_ATTN + 2 * D_MODEL
RMS_EPS = 1e-6
NEG_INF = -1e30
ATTN_SCALE = HEAD_DIM ** -0.5
LOG2E = 1.4426950408889634

COL_U = 0
COL_ZP = D_POOL
COL_Q = 2 * D_POOL
COL_K = COL_Q + D_ATTN
COL_V = COL_K + D_ATTN
COL_ZA = COL_V + D_ATTN
COL_GP = COL_ZA + D_ATTN
COL_GA = COL_GP + D_MODEL

V7X_VMEM_LIMIT_BYTES = 60000 * 1024
COMPILER_SCRATCH_BYTES = 16 << 20
MATMUL_SPILL_BYTES = 4 << 20

QROWS = 4
BAND_ROWS = 12
TQ = QROWS * GRID_W
TK = BAND_ROWS * GRID_W
HEADS_PER_STEP = 4
POOL_HALO = 64
POOL_SUBTILE = 128


def _params(block_bytes, n_axes):
    return pltpu.CompilerParams(
        dimension_semantics=("arbitrary",) * n_axes,
        vmem_limit_bytes=int(min(V7X_VMEM_LIMIT_BYTES, block_bytes + COMPILER_SCRATCH_BYTES)),
    )


def _sigmoid(x):
    return 1.0 / (1.0 + jnp.exp(-x))


def _silu_bf16(z):
    hz = z * 0.5
    return hz + hz * jnp.tanh(hz)


def _split_bf16(v):
    hi = v.astype(BF16)
    lo = (v - hi.astype(F32)).astype(BF16)
    return hi, lo


def _ada_kernel(c_ref, w_ref, b_ref, o_ref):
    cv = c_ref[...]
    s_hi, s_lo = _split_bf16(cv * _sigmoid(cv))
    w_hi, w_lo = _split_bf16(w_ref[...])
    rows = s_hi.shape[0]
    r_hi = jnp.dot(jnp.concatenate([s_hi, s_lo], axis=0), w_hi, preferred_element_type=F32)
    r_lo = jnp.dot(s_hi, w_lo, preferred_element_type=F32)
    o_ref[...] = r_hi[:rows] + r_hi[rows:] + r_lo + b_ref[...]


def _ada_call(cvec, w_ada, b_ada):
    depth, d, n = w_ada.shape
    tn = 512
    blk = 2 * (d * tn * 4) + 2 * 8 * d * 4 + 4 * 8 * tn * 4
    return pl.pallas_call(
        _ada_kernel,
        grid=(depth, n // tn),
        in_specs=[
            pl.BlockSpec((8, d), lambda l, j: (0, 0)),
            pl.BlockSpec((None, d, tn), lambda l, j: (l, 0, j)),
            pl.BlockSpec((None, 1, tn), lambda l, j: (l, 0, j)),
        ],
        out_specs=pl.BlockSpec((None, 8, tn), lambda l, j: (l, 0, j)),
        out_shape=jax.ShapeDtypeStruct((depth, 8, n), F32),
        compiler_params=_params(blk, 2),
        name="ada",
    )(cvec, w_ada, b_ada.reshape(depth, 1, n))


def _mod_kernel(x_ref, g_ref, sh_ref, sc_ref, o_ref):
    x = x_ref[...]
    ms = jnp.mean(x * x, axis=-1, keepdims=True)
    y = x * lax.rsqrt(ms + RMS_EPS) * g_ref[...]
    o_ref[...] = (y * (1.0 + sc_ref[0]) + sh_ref[0]).astype(o_ref.dtype)


def _mod_call(x2, g, sh, sc, row_of_tile, tr):
    m, d = x2.shape
    vec = pl.BlockSpec((1, 1, d), lambda i: (row_of_tile(i), 0, 0))
    return pl.pallas_call(
        _mod_kernel,
        grid=(m // tr,),
        in_specs=[
            pl.BlockSpec((tr, d), lambda i: (i, 0)),
            pl.BlockSpec((1, d), lambda i: (0, 0)),
            vec, vec,
        ],
        out_specs=pl.BlockSpec((tr, d), lambda i: (i, 0)),
        out_shape=jax.ShapeDtypeStruct((m, d), BF16),
        compiler_params=_params(2 * tr * d * 6 + 3 * tr * d * 4, 1),
        name="modulate",
    )(x2, g.reshape(1, d), sh, sc)


def _rms_kernel(x_ref, g_ref, o_ref):
    x = x_ref[...]
    ms = jnp.mean(x * x, axis=-1, keepdims=True)
    o_ref[...] = x * lax.rsqrt(ms + RMS_EPS) * g_ref[...]


def _rms_call(x2, g, tr):
    m, d = x2.shape
    return pl.pallas_call(
        _rms_kernel,
        grid=(m // tr,),
        in_specs=[pl.BlockSpec((tr, d), lambda i: (i, 0)),
                  pl.BlockSpec((1, d), lambda i: (0, 0))],
        out_specs=pl.BlockSpec((tr, d), lambda i: (i, 0)),
        out_shape=jax.ShapeDtypeStruct((m, d), F32),
        compiler_params=_params(2 * tr * d * 8 + 2 * tr * d * 4, 1),
        name="final_norm",
    )(x2, g.reshape(1, d))


def _proj_kernel(a_ref, w_ref, b_ref, o_ref):
    acc = jnp.dot(a_ref[...], w_ref[...].astype(BF16), preferred_element_type=F32)
    o_ref[...] = (acc + b_ref[...]).astype(o_ref.dtype)


def _proj_call(a, w_stack, b_stack, layer, col0, n, tm, tn):
    m, k = a.shape
    depth, _, n_all = w_stack.shape
    jb = col0 // tn
    rows = tm * k * 2
    rest = 2 * (k * tn * 4 + tm * tn * 2 + tn * 4) + MATMUL_SPILL_BYTES
    row_buffers = 2 if 2 * rows + rest <= V7X_VMEM_LIMIT_BYTES else 1
    blk = row_buffers * rows + rest
    tmp = 0
    return pl.pallas_call(
        _proj_kernel,
        grid=(m // tm, n // tn),
        in_specs=[
            pl.BlockSpec((tm, k), lambda i, j: (i, 0), pipeline_mode=pl.Buffered(row_buffers)),
            pl.BlockSpec((None, k, tn), lambda i, j: (layer, 0, jb + j)),
            pl.BlockSpec((None, 1, tn), lambda i, j: (layer, 0, jb + j)),
        ],
        out_specs=pl.BlockSpec((tm, tn), lambda i, j: (i, j)),
        out_shape=jax.ShapeDtypeStruct((m, n), BF16),
        compiler_params=_params(blk + tmp, 2),
        name="in_proj",
    )(a, w_stack, b_stack.reshape(depth, 1, n_all))


def _dot_nt(a, b):
    return lax.dot_general(a, b, (((1,), (1,)), ((), ())), preferred_element_type=F32)


def _scaled_q(q_bf16):
    return (q_bf16.astype(F32) * (ATTN_SCALE * LOG2E)).astype(BF16)


def _with_ones(v):
    return jnp.concatenate([v, jnp.ones_like(v)], axis=1)


def _softmax_pv(scores, values, out_dtype):
    mx = functools.reduce(jnp.maximum, [jnp.max(s, axis=-1, keepdims=True) for s in scores])
    acc = functools.reduce(jnp.add, [jnp.dot(jnp.exp2(s - mx).astype(BF16), _with_ones(v), preferred_element_type=F32)
                                     for s, v in zip(scores, values)])
    return (acc[:, :HEAD_DIM] / acc[:, HEAD_DIM:]).astype(out_dtype)


def _band_start(g, n_rows):
    kr0 = jnp.clip(QROWS * g - NA_KH // 2, 0, n_rows - BAND_ROWS)
    return pl.multiple_of(kr0 * GRID_W, GRID_W)


def _na_scores(q_ref, k_ref, kc_ref, tab_ref, dst_ref, start):
    for hh in range(HEADS_PER_STEP):
        lanes = slice(hh * HEAD_DIM, (hh + 1) * HEAD_DIM)
        q = _scaled_q(q_ref[:, lanes])
        dst_ref[hh, :, :TK] = _dot_nt(q, k_ref[pl.ds(start, TK), lanes]) + tab_ref[hh]
        dst_ref[hh, :, TK:] = _dot_nt(q, kc_ref[:, lanes])


def _na_kernel(q_ref, k_ref, v_ref, kc_ref, vc_ref, za_ref, c2_ref, o_ref, tab_ref, sa_ref, sb_ref,
               *, n_groups, n_rows):
    step_id = pl.program_id(2)
    last = n_groups - 1
    g = jnp.minimum(step_id, last)

    @pl.when(jnp.logical_or(g <= 1, g == last))
    def _():
        interior = jnp.logical_and(g > 0, g < last)
        lo_a = jnp.where(g == last, BAND_ROWS - NA_KH, 0)
        lo_b = jnp.where(interior, 1, 0)
        off = jnp.where(g == 0, NA_KH - 1,
                        jnp.where(g == last, NA_KH - 1 - BAND_ROWS + QROWS, NA_KH - 1 - NA_KH // 2))
        qrow = jnp.right_shift(lax.broadcasted_iota(jnp.int32, (TQ, 1), 0), LOG2_GRID_W)
        lo = lo_a + lo_b * qrow
        jrow = jnp.right_shift(lax.broadcasted_iota(jnp.int32, (1, TK), 1), LOG2_GRID_W)
        row_ok = jnp.logical_and(jrow >= lo, jrow < lo + NA_KH)
        for hh in range(HEADS_PER_STEP):
            bias = jnp.concatenate(
                [jnp.concatenate([c2_ref[hh, 2 * m - i + off + QROWS] for m in range(BAND_ROWS // 2)], axis=1)
                 for i in range(QROWS)], axis=0)
            tab_ref[hh] = jnp.where(row_ok, bias * LOG2E, NEG_INF)

    @pl.when(step_id == 0)
    def _():
        _na_scores(q_ref, k_ref, kc_ref, tab_ref, sb_ref, _band_start(0, n_rows))

    kstart = _band_start(g, n_rows)
    vstart = _band_start(jnp.maximum(step_id - 1, 0), n_rows)

    def step(src_ref, dst_ref):
        _na_scores(q_ref, k_ref, kc_ref, tab_ref, dst_ref, kstart)
        for hh in range(HEADS_PER_STEP):
            lanes = slice(hh * HEAD_DIM, (hh + 1) * HEAD_DIM)
            y = _softmax_pv((src_ref[hh, :, :TK], src_ref[hh, :, TK:]),
                            (v_ref[pl.ds(vstart, TK), lanes], vc_ref[:, lanes]), o_ref.dtype)
            o_ref[:, lanes] = y * _silu_bf16(za_ref[:, lanes])

    @pl.when(step_id % 2 == 0)
    def _():
        step(sb_ref, sa_ref)

    @pl.when(step_id % 2 == 1)
    def _():
        step(sa_ref, sb_ref)


def _na_call(p, pc, c2, layer, batch, seq, ctx_len, kc_col, vc_col):
    n_rows = seq // GRID_W
    n_groups = n_rows // QROWS
    last = n_groups - 1
    wb = HEADS_PER_STEP * HEAD_DIM
    n_c2 = c2.shape[2]
    c2_bytes = HEADS_PER_STEP * n_c2 * GRID_W * 2 * GRID_W * 4
    tab_bytes = HEADS_PER_STEP * TQ * TK * 4
    score_bytes = HEADS_PER_STEP * TQ * (TK + ctx_len) * 4
    blk = 2 * (TQ * wb * 2 * 3 + 2 * seq * wb * 2 + 2 * ctx_len * wb * 2 + c2_bytes)
    kern = functools.partial(_na_kernel, n_groups=n_groups, n_rows=n_rows)
    return pl.pallas_call(
        kern,
        grid=(batch, N_HEADS // HEADS_PER_STEP, n_groups + 1),
        in_specs=[
            pl.BlockSpec((TQ, wb), lambda b, h, s: (b * n_groups + jnp.minimum(s, last), COL_Q // wb + h)),
            pl.BlockSpec((seq, wb), lambda b, h, s: (b, COL_K // wb + h)),
            pl.BlockSpec((seq, wb), lambda b, h, s: (b, COL_V // wb + h)),
            pl.BlockSpec((ctx_len, wb), lambda b, h, s: (b, kc_col // wb + h)),
            pl.BlockSpec((ctx_len, wb), lambda b, h, s: (b, vc_col // wb + h)),
            pl.BlockSpec((TQ, wb), lambda b, h, s: (b * n_groups + jnp.maximum(s - 1, 0), COL_ZA // wb + h)),
            pl.BlockSpec((None, HEADS_PER_STEP, n_c2, GRID_W, 2 * GRID_W), lambda b, h, s: (layer, h, 0, 0, 0)),
        ],
        out_specs=pl.BlockSpec((TQ, wb), lambda b, h, s: (b * n_groups + jnp.maximum(s - 1, 0), h)),
        out_shape=jax.ShapeDtypeStruct((batch * seq, D_ATTN), BF16),
        scratch_shapes=[pltpu.VMEM((HEADS_PER_STEP, TQ, TK), F32),
                        pltpu.VMEM((HEADS_PER_STEP, TQ, TK + ctx_len), F32),
                        pltpu.VMEM((HEADS_PER_STEP, TQ, TK + ctx_len), F32)],
        compiler_params=_params(blk + tab_bytes + 2 * score_bytes, 3),
        name="na_attn",
    )(p, p, p, pc, pc, p, c2)


def _ctx_attn_kernel(q_ref, k_ref, v_ref, za_ref, o_ref):
    for hh in range(HEADS_PER_STEP):
        lanes = slice(hh * HEAD_DIM, (hh + 1) * HEAD_DIM)
        s = _dot_nt(_scaled_q(q_ref[:, lanes]), k_ref[:, lanes])
        o_ref[:, lanes] = _softmax_pv((s,), (v_ref[:, lanes],), o_ref.dtype) * _silu_bf16(za_ref[:, lanes])


def _ctx_attn_call(pc, batch, ctx_len):
    hb = HEADS_PER_STEP * HEAD_DIM
    return pl.pallas_call(
        _ctx_attn_kernel,
        grid=(batch, N_HEADS // HEADS_PER_STEP),
        in_specs=[
            pl.BlockSpec((ctx_len, hb), lambda b, h: (b, COL_Q // hb + h)),
            pl.BlockSpec((ctx_len, hb), lambda b, h: (b, COL_K // hb + h)),
            pl.BlockSpec((ctx_len, hb), lambda b, h: (b, COL_V // hb + h)),
            pl.BlockSpec((ctx_len, hb), lambda b, h: (b, COL_ZA // hb + h)),
        ],
        out_specs=pl.BlockSpec((ctx_len, hb), lambda b, h: (b, h)),
        out_shape=jax.ShapeDtypeStruct((batch * ctx_len, D_ATTN), BF16),
        compiler_params=_params(8 * ctx_len * hb * 2 + 8 * ctx_len * ctx_len * 4, 2),
        name="ctx_attn",
    )(pc, pc, pc, pc)


def _bias_pair_table(rpb):
    qc = np.arange(GRID_W)[:, None]
    kc = np.arange(GRID_W)[None, :]
    ws = np.clip(qc - NA_KW // 2, 0, GRID_W - NA_KW)
    col_ok = (kc >= ws) & (kc < ws + NA_KW)
    dc = kc - qc + NA_KW - 1
    n_dr, n_dc = 2 * NA_KH - 1, 2 * NA_KW - 1
    onehot = ((dc[None] == np.arange(n_dc)[:, None, None]) & col_ok[None]).astype(np.float32)
    n_e = BAND_ROWS + NA_KH - 1 + QROWS - 1
    pair_rows = np.stack([np.clip(np.arange(n_e) - QROWS, 0, n_dr - 1),
                          np.clip(np.arange(n_e) - QROWS + 1, 0, n_dr - 1)], axis=1)
    n_l, n_h = rpb.shape[:2]
    lhs = jnp.concatenate([rpb[:, :, pair_rows, :].reshape(n_l * n_h * n_e, 2 * n_dc),
                           jnp.ones((n_l * n_h * n_e, 1), F32)], axis=1)
    maps = np.zeros((2, n_dc, GRID_W, 2 * GRID_W), np.float32)
    maps[0, :, :, :GRID_W] = onehot
    maps[1, :, :, GRID_W:] = onehot
    mask_row = np.tile(np.where(col_ok, 0.0, NEG_INF).astype(np.float32), (1, 2))
    rhs = np.concatenate([maps.reshape(2 * n_dc, -1), mask_row.reshape(1, -1)], axis=0)
    table = jnp.dot(lhs, jnp.asarray(rhs), precision=lax.Precision.HIGHEST)
    return table.reshape(n_l, n_h, n_e, GRID_W, 2 * GRID_W)


def _pool_bands():
    r = np.arange(POOL_SUBTILE)[:, None]
    c = np.arange(POOL_SUBTILE + 2 * POOL_HALO)[None, :] - POOL_HALO
    return np.stack([(c >= r - w // 2) & (c <= r - w // 2 + w - 1) for w in POOL_WINDOWS]).astype(np.float32)


def _pool_gate_kernel(ucur_ref, uprev_ref, unext_ref, zp_ref, band_ref, wp_ref, sp_ref, o_ref,
                      *, tiles_per_seq, seq):
    t = ucur_ref.shape[0]
    cg = D_POOL_GROUP
    ts = pl.program_id(0) % tiles_per_seq
    base = ts * t
    uprev = jnp.where(ts > 0, uprev_ref[...], jnp.zeros_like(uprev_ref))
    unext = jnp.where(ts < tiles_per_seq - 1, unext_ref[...], jnp.zeros_like(unext_ref))
    ucat = jnp.concatenate([uprev, ucur_ref[...], unext], axis=0)
    tpos = lax.broadcasted_iota(jnp.int32, (t, 1), 0) + base
    for gi, w in enumerate(POOL_WINDOWS):
        cols = slice(gi * cg, (gi + 1) * cg)
        wsum = jnp.concatenate(
            [jnp.dot(band_ref[gi], ucat[r0:r0 + POOL_SUBTILE + 2 * POOL_HALO, cols], preferred_element_type=F32)
             for r0 in range(0, t, POOL_SUBTILE)], axis=0)
        cnt = (jnp.minimum(tpos - w // 2 + w - 1, seq - 1) - jnp.maximum(tpos - w // 2, 0) + 1).astype(F32)
        pooled = wsum / cnt - ucur_ref[:, cols].astype(F32)
        y = jnp.dot(pooled.astype(BF16), wp_ref[gi].astype(BF16), preferred_element_type=F32) * sp_ref[:, cols]
        o_ref[:, cols] = y.astype(BF16) * _silu_bf16(zp_ref[:, cols])


def _pool_gate_call(p, w_pool, s_pool_l, layer, seq, t):
    m = p.shape[0]
    tiles_per_seq = seq // t
    hpt = t // POOL_HALO
    n_halo_blocks = m // POOL_HALO
    wp_shape = w_pool.shape[1:]
    bands = jnp.asarray(_pool_bands(), BF16)
    blk = 2 * (3 * t * D_POOL * 2 + 2 * POOL_HALO * D_POOL * 2 + int(np.prod(wp_shape)) * 4 + D_POOL * 4
               + bands.size * 2)
    tmp = 8 * t * D_POOL_GROUP * 4
    kern = functools.partial(_pool_gate_kernel, tiles_per_seq=tiles_per_seq, seq=seq)
    return pl.pallas_call(
        kern,
        grid=(m // t,),
        in_specs=[
            pl.BlockSpec((t, D_POOL), lambda i: (i, COL_U // D_POOL)),
            pl.BlockSpec((POOL_HALO, D_POOL), lambda i: (jnp.maximum(i * hpt - 1, 0), COL_U // D_POOL)),
            pl.BlockSpec((POOL_HALO, D_POOL),
                         lambda i: (jnp.minimum((i + 1) * hpt, n_halo_blocks - 1), COL_U // D_POOL)),
            pl.BlockSpec((t, D_POOL), lambda i: (i, COL_ZP // D_POOL)),
            pl.BlockSpec(bands.shape, lambda i: (0, 0, 0)),
            pl.BlockSpec((None,) + wp_shape, lambda i: (layer, 0, 0, 0)),
            pl.BlockSpec((1, D_POOL), lambda i: (0, 0)),
        ],
        out_specs=pl.BlockSpec((t, D_POOL), lambda i: (i, 0)),
        out_shape=jax.ShapeDtypeStruct((m, D_POOL), BF16),
        compiler_params=_params(blk + tmp, 1),
        name="pool_gate",
    )(p, p, p, p, bands, w_pool, s_pool_l.reshape(1, D_POOL))


def _sigmoid_bf16(z):
    return 0.5 + 0.5 * jnp.tanh(z * 0.5)


def _merge_kernel(ap_ref, aa_ref, wp_ref, wa_ref, gp_ref, ga_ref, o_ref):
    br_p = jnp.dot(ap_ref[...], wp_ref[...].astype(BF16), preferred_element_type=F32)
    br_a = jnp.dot(aa_ref[...], wa_ref[...].astype(BF16), preferred_element_type=F32)
    gp = _sigmoid_bf16(gp_ref[...]).astype(F32)
    ga = _sigmoid_bf16(ga_ref[...]).astype(F32)
    o_ref[...] = (gp * br_p + ga * br_a).astype(o_ref.dtype)


def _merge_call(a_pool, a_attn, w_br_pool, w_br_attn, p, layer, tm, tn):
    m = a_pool.shape[0]
    n = D_MODEL
    blk = 2 * (tm * D_MODEL * 2 + 2 * D_POOL * tn * 4 + 3 * tm * tn * 2)
    tmp = 2 * tm * tn * 4
    return pl.pallas_call(
        _merge_kernel,
        grid=(m // tm, n // tn),
        in_specs=[
            pl.BlockSpec((tm, D_POOL), lambda i, j: (i, 0)),
            pl.BlockSpec((tm, D_ATTN), lambda i, j: (i, 0)),
            pl.BlockSpec((None, D_POOL, tn), lambda i, j: (layer, 0, j)),
            pl.BlockSpec((None, D_ATTN, tn), lambda i, j: (layer, 0, j)),
            pl.BlockSpec((tm, tn), lambda i, j: (i, COL_GP // tn + j)),
            pl.BlockSpec((tm, tn), lambda i, j: (i, COL_GA // tn + j)),
        ],
        out_specs=pl.BlockSpec((tm, tn), lambda i, j: (i, j)),
        out_shape=jax.ShapeDtypeStruct((m, n), BF16),
        compiler_params=_params(blk + tmp, 2),
        name="merge",
    )(a_pool, a_attn, w_br_pool, w_br_attn, p, p)


def _out_kernel(m_ref, w_ref, x_ref, gt_ref, o_ref):
    acc = jnp.dot(m_ref[...], w_ref[...].astype(BF16), preferred_element_type=F32)
    o_ref[...] = x_ref[...] + gt_ref[0] * acc


def _out_call(mix, w_out, x2, gt, layer, row_of_tile, tm, tn):
    m, k = mix.shape
    n = w_out.shape[2]
    blk = tm * k * 2 + 2 * (k * tn * 4 + 2 * tm * tn * 4 + tn * 4)
    tmp = tm * tn * 4
    return pl.pallas_call(
        _out_kernel,
        grid=(m // tm, n // tn),
        in_specs=[
            pl.BlockSpec((tm, k), lambda i, j: (i, 0), pipeline_mode=pl.Buffered(1)),
            pl.BlockSpec((None, k, tn), lambda i, j: (layer, 0, j)),
            pl.BlockSpec((tm, tn), lambda i, j: (i, j)),
            pl.BlockSpec((1, 1, tn), lambda i, j: (row_of_tile(i), 0, j)),
        ],
        out_specs=pl.BlockSpec((tm, tn), lambda i, j: (i, j)),
        out_shape=jax.ShapeDtypeStruct((m, n), F32),
        compiler_params=_params(blk + tmp, 2),
        name="out_proj",
    )(mix, w_out, x2, gt)


def kernel(x, c, ctx, c_ctx, norm_g, w_ada, b_ada, w_in, b_in, w_pool, s_pool, rpb,
           w_br_pool, w_br_attn, w_out, final_g):
    batch, seq, d = x.shape
    ctx_len = ctx.shape[1]
    depth = w_in.shape[0]
    ctx_row = batch

    x_lat = x.reshape(batch * seq, d)
    x_ctx = ctx.reshape(batch * ctx_len, d)
    cvec = jnp.zeros((8, d), F32).at[:batch].set(c).at[ctx_row].set(c_ctx)
    ada = _ada_call(cvec, w_ada, b_ada)
    c2 = _bias_pair_table(rpb)

    tm_in, tn_in = 2048, 512
    tm_out, tn_out = 2048, 512
    tm_mrg, tn_mrg = 1024, 512
    tm_ctx = batch * ctx_len
    tn_ctx = 1024
    t_row = 256
    t_norm = 512
    lat_row = lambda tile_rows: (lambda i: i // (seq // tile_rows))
    ctx_row_fn = lambda i: ctx_row

    for l in range(depth):
        last = l == depth - 1
        mod = ada[l].reshape(8, 3, 1, d)
        sh, sc, gt = mod[:, 0], mod[:, 1], mod[:, 2]

        h_lat = _mod_call(x_lat, norm_g[l], sh, sc, lat_row(t_norm), t_norm)
        h_ctx = _mod_call(x_ctx, norm_g[l], sh, sc, ctx_row_fn, t_norm)
        p_lat = _proj_call(h_lat, w_in, b_in, l, 0, D_IN, tm_in, tn_in)
        if last:
            p_ctx = _proj_call(h_ctx, w_in, b_in, l, COL_K, 2 * D_ATTN, tm_ctx, tn_in)
            kc_col, vc_col = 0, D_ATTN
        else:
            p_ctx = _proj_call(h_ctx, w_in, b_in, l, 0, D_IN, tm_ctx, tn_ctx)
            kc_col, vc_col = COL_K, COL_V

        a_attn = _na_call(p_lat, p_ctx, c2, l, batch, seq, ctx_len, kc_col, vc_col)
        a_pool = _pool_gate_call(p_lat, w_pool, s_pool[l], l, seq, t_row)
        mix = _merge_call(a_pool, a_attn, w_br_pool, w_br_attn, p_lat, l, tm_mrg, tn_mrg)
        x_lat_new = _out_call(mix, w_out, x_lat, gt, l, lat_row(tm_out), tm_out, tn_out)

        if not last:
            a_attn_c = _ctx_attn_call(p_ctx, batch, ctx_len)
            a_pool_c = _pool_gate_call(p_ctx, w_pool, s_pool[l], l, ctx_len, t_row)
            mix_c = _merge_call(a_pool_c, a_attn_c, w_br_pool, w_br_attn, p_ctx, l, tm_ctx, tn_mrg)
            x_ctx = _out_call(mix_c, w_out, x_ctx, gt, l, ctx_row_fn, tm_ctx, tn_out)
        x_lat = x_lat_new

    return _rms_call(x_lat, final_g, t_norm).reshape(batch, seq, d)
```

```python
import functools

import numpy as np
import jax
import jax.numpy as jnp
from jax import lax
from jax.experimental import pallas as pl
from jax.experimental.pallas import tpu as pltpu

F32 = jnp.float32
BF16 = jnp.bfloat16

D_MODEL = 4096
GRID_W = 64
LOG2_GRID_W = 6
D_POOL = D_MODEL // 2
POOL_WINDOWS = (2, 4, 8, 16)
D_POOL_GROUP = D_POOL // len(POOL_WINDOWS)
HEAD_DIM = 128
D_ATTN = D_MODEL // 2
N_HEADS = D_ATTN // HEAD_DIM
NA_KH = 8
NA_KW = 16
D_IN = 2 * D_POOL + 4 * D_ATTN + 2 * D_MODEL
RMS_EPS = 1e-6
NEG_INF = -1e30
ATTN_SCALE = HEAD_DIM ** -0.5
LOG2E = 1.4426950408889634

COL_U = 0
COL_ZP = D_POOL
COL_Q = 2 * D_POOL
COL_K = COL_Q + D_ATTN
COL_V = COL_K + D_ATTN
COL_ZA = COL_V + D_ATTN
COL_GP = COL_ZA + D_ATTN
COL_GA = COL_GP + D_MODEL

V7X_VMEM_LIMIT_BYTES = 60000 * 1024
COMPILER_SCRATCH_BYTES = 16 << 20
MATMUL_SPILL_BYTES = 4 << 20

QROWS = 4
BAND_ROWS = 12
TQ = QROWS * GRID_W
TK = BAND_ROWS * GRID_W
HEADS_PER_STEP = 4
CAST_ROWS, CAST_COLS = 512, 1024
POOL_HALO = 64
POOL_SUBTILE = 128


def _params(block_bytes, n_axes):
    return pltpu.CompilerParams(
        dimension_semantics=("arbitrary",) * n_axes,
        vmem_limit_bytes=int(min(V7X_VMEM_LIMIT_BYTES, block_bytes + COMPILER_SCRATCH_BYTES)),
    )


def _sigmoid(x):
    return 1.0 / (1.0 + jnp.exp(-x))


def _silu_bf16(z):
    hz = z * 0.5
    return hz + hz * jnp.tanh(hz)


def _split_bf16(v):
    hi = v.astype(BF16)
    lo = (v - hi.astype(F32)).astype(BF16)
    return hi, lo


def _ada_kernel(c_ref, w_ref, b_ref, o_ref):
    cv = c_ref[...]
    s_hi, s_lo = _split_bf16(cv * _sigmoid(cv))
    w_hi, w_lo = _split_bf16(w_ref[...])
    rows = s_hi.shape[0]
    r_hi = jnp.dot(jnp.concatenate([s_hi, s_lo], axis=0), w_hi, preferred_element_type=F32)
    r_lo = jnp.dot(s_hi, w_lo, preferred_element_type=F32)
    o_ref[...] = r_hi[:rows] + r_hi[rows:] + r_lo + b_ref[...]


def _ada_call(cvec, w_ada, b_ada):
    depth, d, n = w_ada.shape
    tn = 512
    blk = 2 * (d * tn * 4) + 2 * 8 * d * 4 + 4 * 8 * tn * 4
    return pl.pallas_call(
        _ada_kernel,
        grid=(depth, n // tn),
        in_specs=[
            pl.BlockSpec((8, d), lambda l, j: (0, 0)),
            pl.BlockSpec((None, d, tn), lambda l, j: (l, 0, j)),
            pl.BlockSpec((None, 1, tn), lambda l, j: (l, 0, j)),
        ],
        out_specs=pl.BlockSpec((None, 8, tn), lambda l, j: (l, 0, j)),
        out_shape=jax.ShapeDtypeStruct((depth, 8, n), F32),
        compiler_params=_params(blk, 2),
        name="ada",
    )(cvec, w_ada, b_ada.reshape(depth, 1, n))


def _mod_kernel(x_ref, g_ref, sh_ref, sc_ref, o_ref):
    x = x_ref[...]
    ms = jnp.mean(x * x, axis=-1, keepdims=True)
    y = x * lax.rsqrt(ms + RMS_EPS) * g_ref[...]
    o_ref[...] = (y * (1.0 + sc_ref[0]) + sh_ref[0]).astype(o_ref.dtype)


def _mod_call(x2, g, sh, sc, row_of_tile, tr):
    m, d = x2.shape
    vec = pl.BlockSpec((1, 1, d), lambda i: (row_of_tile(i), 0, 0))
    return pl.pallas_call(
        _mod_kernel,
        grid=(m // tr,),
        in_specs=[
            pl.BlockSpec((tr, d), lambda i: (i, 0)),
            pl.BlockSpec((1, d), lambda i: (0, 0)),
            vec, vec,
        ],
        out_specs=pl.BlockSpec((tr, d), lambda i: (i, 0)),
        out_shape=jax.ShapeDtypeStruct((m, d), BF16),
        compiler_params=_params(2 * tr * d * 6 + 3 * tr * d * 4, 1),
        name="modulate",
    )(x2, g.reshape(1, d), sh, sc)


def _rms_kernel(x_ref, g_ref, o_ref):
    x = x_ref[...]
    ms = jnp.mean(x * x, axis=-1, keepdims=True)
    o_ref[...] = x * lax.rsqrt(ms + RMS_EPS) * g_ref[...]


def _rms_call(x2, g, tr):
    m, d = x2.shape
    return pl.pallas_call(
        _rms_kernel,
        grid=(m // tr,),
        in_specs=[pl.BlockSpec((tr, d), lambda i: (i, 0)),
                  pl.BlockSpec((1, d), lambda i: (0, 0))],
        out_specs=pl.BlockSpec((tr, d), lambda i: (i, 0)),
        out_shape=jax.ShapeDtypeStruct((m, d), F32),
        compiler_params=_params(2 * tr * d * 8 + 2 * tr * d * 4, 1),
        name="final_norm",
    )(x2, g.reshape(1, d))


def _proj_kernel(a_ref, w_ref, b_ref, o_ref):
    acc = jnp.dot(a_ref[...], w_ref[...].astype(BF16), preferred_element_type=F32)
    o_ref[...] = (acc + b_ref[...]).astype(o_ref.dtype)


def _proj_call(a, w_stack, b_stack, layer, col0, n, tm, tn):
    m, k = a.shape
    depth, _, n_all = w_stack.shape
    jb = col0 // tn
    rows = tm * k * 2
    rest = 2 * (k * tn * 4 + tm * tn * 2 + tn * 4) + MATMUL_SPILL_BYTES
    row_buffers = 2 if 2 * rows + rest <= V7X_VMEM_LIMIT_BYTES else 1
    blk = row_buffers * rows + rest
    tmp = 0
    return pl.pallas_call(
        _proj_kernel,
        grid=(m // tm, n // tn),
        in_specs=[
            pl.BlockSpec((tm, k), lambda i, j: (i, 0), pipeline_mode=pl.Buffered(row_buffers)),
            pl.BlockSpec((None, k, tn), lambda i, j: (layer, 0, jb + j)),
            pl.BlockSpec((None, 1, tn), lambda i, j: (layer, 0, jb + j)),
        ],
        out_specs=pl.BlockSpec((tm, tn), lambda i, j: (i, j)),
        out_shape=jax.ShapeDtypeStruct((m, n), BF16),
        compiler_params=_params(blk + tmp, 2),
        name="in_proj",
    )(a, w_stack, b_stack.reshape(depth, 1, n_all))


def _dot_nt(a, b):
    return lax.dot_general(a, b, (((1,), (1,)), ((), ())), preferred_element_type=F32)


def _scaled_q(q_bf16):
    return (q_bf16.astype(F32) * (ATTN_SCALE * LOG2E)).astype(BF16)


def _with_ones(v):
    return jnp.concatenate([v, jnp.ones_like(v)], axis=1)


def _softmax_pv(scores, values, out_dtype):
    mx = functools.reduce(jnp.maximum, [jnp.max(s, axis=-1, keepdims=True) for s in scores])
    acc = functools.reduce(jnp.add, [jnp.dot(jnp.exp2(s - mx).astype(BF16), _with_ones(v), preferred_element_type=F32)
                                     for s, v in zip(scores, values)])
    return (acc[:, :HEAD_DIM] / acc[:, HEAD_DIM:]).astype(out_dtype)


def _band_start(g, n_rows):
    kr0 = jnp.clip(QROWS * g - NA_KH // 2, 0, n_rows - BAND_ROWS)
    return pl.multiple_of(kr0 * GRID_W, GRID_W)


def _na_scores(q_ref, k_ref, kc_ref, tab_ref, dst_ref, start):
    for hh in range(HEADS_PER_STEP):
        lanes = slice(hh * HEAD_DIM, (hh + 1) * HEAD_DIM)
        q = _scaled_q(q_ref[:, lanes])
        dst_ref[hh, :, :TK] = _dot_nt(q, k_ref[pl.ds(start, TK), lanes]) + tab_ref[hh]
        dst_ref[hh, :, TK:] = _dot_nt(q, kc_ref[:, lanes])


def _cast_tile_index(t, first, n_col_blocks, n_tiles):
    k = jnp.clip(t - first, 0, n_tiles - 1)
    return k // n_col_blocks, k % n_col_blocks


def _na_kernel(q_ref, k_ref, v_ref, kc_ref, vc_ref, za_ref, c2_ref, wp_ref, wa_ref, wo_ref,
               o_ref, wp_out, wa_out, wo_out, tab_ref, sa_ref, sb_ref, *, n_groups, n_rows, cast_plan):
    step_id = pl.program_id(2)
    t_lin = (pl.program_id(0) * pl.num_programs(1) + pl.program_id(1)) * pl.num_programs(2) + step_id
    for (first, n_tiles), src, dst in zip(cast_plan, (wp_ref, wa_ref, wo_ref), (wp_out, wa_out, wo_out)):
        @pl.when(jnp.logical_and(t_lin >= first, t_lin < first + n_tiles))
        def _(src=src, dst=dst):
            dst[...] = src[...].astype(dst.dtype)

    last = n_groups - 1
    g = jnp.minimum(step_id, last)

    @pl.when(jnp.logical_or(g <= 1, g == last))
    def _():
        interior = jnp.logical_and(g > 0, g < last)
        lo_a = jnp.where(g == last, BAND_ROWS - NA_KH, 0)
        lo_b = jnp.where(interior, 1, 0)
        off = jnp.where(g == 0, NA_KH - 1,
                        jnp.where(g == last, NA_KH - 1 - BAND_ROWS + QROWS, NA_KH - 1 - NA_KH // 2))
        qrow = jnp.right_shift(lax.broadcasted_iota(jnp.int32, (TQ, 1), 0), LOG2_GRID_W)
        lo = lo_a + lo_b * qrow
        jrow = jnp.right_shift(lax.broadcasted_iota(jnp.int32, (1, TK), 1), LOG2_GRID_W)
        row_ok = jnp.logical_and(jrow >= lo, jrow < lo + NA_KH)
        for hh in range(HEADS_PER_STEP):
            bias = jnp.concatenate(
                [jnp.concatenate([c2_ref[hh, 2 * m - i + off + QROWS] for m in range(BAND_ROWS // 2)], axis=1)
                 for i in range(QROWS)], axis=0)
            tab_ref[hh] = jnp.where(row_ok, bias * LOG2E, NEG_INF)

    @pl.when(step_id == 0)
    def _():
        _na_scores(q_ref, k_ref, kc_ref, tab_ref, sb_ref, _band_start(0, n_rows))

    kstart = _band_start(g, n_rows)
    vstart = _band_start(jnp.maximum(step_id - 1, 0), n_rows)

    def step(src_ref, dst_ref):
        _na_scores(q_ref, k_ref, kc_ref, tab_ref, dst_ref, kstart)
        for hh in range(HEADS_PER_STEP):
            lanes = slice(hh * HEAD_DIM, (hh + 1) * HEAD_DIM)
            y = _softmax_pv((src_ref[hh, :, :TK], src_ref[hh, :, TK:]),
                            (v_ref[pl.ds(vstart, TK), lanes], vc_ref[:, lanes]), o_ref.dtype)
            o_ref[:, lanes] = y * _silu_bf16(za_ref[:, lanes])

    @pl.when(step_id % 2 == 0)
    def _():
        step(sb_ref, sa_ref)

    @pl.when(step_id % 2 == 1)
    def _():
        step(sa_ref, sb_ref)


def _na_call(p, pc, c2, w_br_pool, w_br_attn, w_out, layer, batch, seq, ctx_len, kc_col, vc_col):
    n_rows = seq // GRID_W
    n_groups = n_rows // QROWS
    last = n_groups - 1
    wb = HEADS_PER_STEP * HEAD_DIM
    n_c2 = c2.shape[2]
    c2_bytes = HEADS_PER_STEP * n_c2 * GRID_W * 2 * GRID_W * 4
    tab_bytes = HEADS_PER_STEP * TQ * TK * 4
    score_bytes = HEADS_PER_STEP * TQ * (TK + ctx_len) * 4
    blk = 2 * (TQ * wb * 2 * 3 + 2 * seq * wb * 2 + 2 * ctx_len * wb * 2 + c2_bytes
               + 3 * CAST_ROWS * CAST_COLS * 6)
    n_hg = N_HEADS // HEADS_PER_STEP
    steps = n_groups + 1
    weights = (w_br_pool, w_br_attn, w_out)
    tiles = [(w.shape[1] // CAST_ROWS, w.shape[2] // CAST_COLS) for w in weights]
    firsts = np.cumsum([0] + [r * c for r, c in tiles])
    assert firsts[-1] <= batch * n_hg * steps
    cast_plan = tuple((int(f), r * c) for f, (r, c) in zip(firsts, tiles))

    def lin(b, h, s):
        return (b * n_hg + h) * steps + s

    def cast_in_spec(i):
        (first, n_tiles), (_, ncb) = cast_plan[i], tiles[i]
        return pl.BlockSpec((None, CAST_ROWS, CAST_COLS),
                            lambda b, h, s: (layer,) + _cast_tile_index(lin(b, h, s), first, ncb, n_tiles))

    def cast_out_spec(i):
        (first, n_tiles), (_, ncb) = cast_plan[i], tiles[i]
        return pl.BlockSpec((CAST_ROWS, CAST_COLS),
                            lambda b, h, s: _cast_tile_index(lin(b, h, s), first, ncb, n_tiles))

    kern = functools.partial(_na_kernel, n_groups=n_groups, n_rows=n_rows, cast_plan=cast_plan)
    return pl.pallas_call(
        kern,
        grid=(batch, N_HEADS // HEADS_PER_STEP, n_groups + 1),
        in_specs=[
            pl.BlockSpec((TQ, wb), lambda b, h, s: (b * n_groups + jnp.minimum(s, last), COL_Q // wb + h)),
            pl.BlockSpec((seq, wb), lambda b, h, s: (b, COL_K // wb + h)),
            pl.BlockSpec((seq, wb), lambda b, h, s: (b, COL_V // wb + h)),
            pl.BlockSpec((ctx_len, wb), lambda b, h, s: (b, kc_col // wb + h)),
            pl.BlockSpec((ctx_len, wb), lambda b, h, s: (b, vc_col // wb + h)),
            pl.BlockSpec((TQ, wb), lambda b, h, s: (b * n_groups + jnp.maximum(s - 1, 0), COL_ZA // wb + h)),
            pl.BlockSpec((None, HEADS_PER_STEP, n_c2, GRID_W, 2 * GRID_W), lambda b, h, s: (layer, h, 0, 0, 0)),
            cast_in_spec(0), cast_in_spec(1), cast_in_spec(2),
        ],
        out_specs=[pl.BlockSpec((TQ, wb), lambda b, h, s: (b * n_groups + jnp.maximum(s - 1, 0), h)),
                   cast_out_spec(0), cast_out_spec(1), cast_out_spec(2)],
        out_shape=[jax.ShapeDtypeStruct((batch * seq, D_ATTN), BF16)]
                  + [jax.ShapeDtypeStruct(w.shape[1:], BF16) for w in weights],
        scratch_shapes=[pltpu.VMEM((HEADS_PER_STEP, TQ, TK), F32),
                        pltpu.VMEM((HEADS_PER_STEP, TQ, TK + ctx_len), F32),
                        pltpu.VMEM((HEADS_PER_STEP, TQ, TK + ctx_len), F32)],
        compiler_params=_params(blk + tab_bytes + 2 * score_bytes, 3),
        name="na_attn",
    )(p, p, p, pc, pc, p, c2, *weights)


def _ctx_attn_kernel(q_ref, k_ref, v_ref, za_ref, o_ref):
    for hh in range(HEADS_PER_STEP):
        lanes = slice(hh * HEAD_DIM, (hh + 1) * HEAD_DIM)
        s = _dot_nt(_scaled_q(q_ref[:, lanes]), k_ref[:, lanes])
        o_ref[:, lanes] = _softmax_pv((s,), (v_ref[:, lanes],), o_ref.dtype) * _silu_bf16(za_ref[:, lanes])


def _ctx_attn_call(pc, batch, ctx_len):
    hb = HEADS_PER_STEP * HEAD_DIM
    return pl.pallas_call(
        _ctx_attn_kernel,
        grid=(batch, N_HEADS // HEADS_PER_STEP),
        in_specs=[
            pl.BlockSpec((ctx_len, hb), lambda b, h: (b, COL_Q // hb + h)),
            pl.BlockSpec((ctx_len, hb), lambda b, h: (b, COL_K // hb + h)),
            pl.BlockSpec((ctx_len, hb), lambda b, h: (b, COL_V // hb + h)),
            pl.BlockSpec((ctx_len, hb), lambda b, h: (b, COL_ZA // hb + h)),
        ],
        out_specs=pl.BlockSpec((ctx_len, hb), lambda b, h: (b, h)),
        out_shape=jax.ShapeDtypeStruct((batch * ctx_len, D_ATTN), BF16),
        compiler_params=_params(8 * ctx_len * hb * 2 + 8 * ctx_len * ctx_len * 4, 2),
        name="ctx_attn",
    )(pc, pc, pc, pc)


def _bias_pair_table(rpb):
    qc = np.arange(GRID_W)[:, None]
    kc = np.arange(GRID_W)[None, :]
    ws = np.clip(qc - NA_KW // 2, 0, GRID_W - NA_KW)
    col_ok = (kc >= ws) & (kc < ws + NA_KW)
    dc = kc - qc + NA_KW - 1
    n_dr, n_dc = 2 * NA_KH - 1, 2 * NA_KW - 1
    onehot = ((dc[None] == np.arange(n_dc)[:, None, None]) & col_ok[None]).astype(np.float32)
    n_e = BAND_ROWS + NA_KH - 1 + QROWS - 1
    pair_rows = np.stack([np.clip(np.arange(n_e) - QROWS, 0, n_dr - 1),
                          np.clip(np.arange(n_e) - QROWS + 1, 0, n_dr - 1)], axis=1)
    n_l, n_h = rpb.shape[:2]
    lhs = jnp.concatenate([rpb[:, :, pair_rows, :].reshape(n_l * n_h * n_e, 2 * n_dc),
                           jnp.ones((n_l * n_h * n_e, 1), F32)], axis=1)
    maps = np.zeros((2, n_dc, GRID_W, 2 * GRID_W), np.float32)
    maps[0, :, :, :GRID_W] = onehot
    maps[1, :, :, GRID_W:] = onehot
    mask_row = np.tile(np.where(col_ok, 0.0, NEG_INF).astype(np.float32), (1, 2))
    rhs = np.concatenate([maps.reshape(2 * n_dc, -1), mask_row.reshape(1, -1)], axis=0)
    table = jnp.dot(lhs, jnp.asarray(rhs), precision=lax.Precision.HIGHEST)
    return table.reshape(n_l, n_h, n_e, GRID_W, 2 * GRID_W)


def _pool_bands():
    r = np.arange(POOL_SUBTILE)[:, None]
    c = np.arange(POOL_SUBTILE + 2 * POOL_HALO)[None, :] - POOL_HALO
    return np.stack([(c >= r - w // 2) & (c <= r - w // 2 + w - 1) for w in POOL_WINDOWS]).astype(np.float32)


def _pool_gate_kernel(ucur_ref, uprev_ref, unext_ref, zp_ref, band_ref, wp_ref, sp_ref, o_ref,
                      *, tiles_per_seq, seq):
    t = ucur_ref.shape[0]
    cg = D_POOL_GROUP
    ts = pl.program_id(0) % tiles_per_seq
    base = ts * t
    uprev = jnp.where(ts > 0, uprev_ref[...], jnp.zeros_like(uprev_ref))
    unext = jnp.where(ts < tiles_per_seq - 1, unext_ref[...], jnp.zeros_like(unext_ref))
    ucat = jnp.concatenate([uprev, ucur_ref[...], unext], axis=0)
    tpos = lax.broadcasted_iota(jnp.int32, (t, 1), 0) + base
    for gi, w in enumerate(POOL_WINDOWS):
        cols = slice(gi * cg, (gi + 1) * cg)
        wsum = jnp.concatenate(
            [jnp.dot(band_ref[gi], ucat[r0:r0 + POOL_SUBTILE + 2 * POOL_HALO, cols], preferred_element_type=F32)
             for r0 in range(0, t, POOL_SUBTILE)], axis=0)
        cnt = (jnp.minimum(tpos - w // 2 + w - 1, seq - 1) - jnp.maximum(tpos - w // 2, 0) + 1).astype(F32)
        pooled = wsum / cnt - ucur_ref[:, cols].astype(F32)
        y = jnp.dot(pooled.astype(BF16), wp_ref[gi].astype(BF16), preferred_element_type=F32) * sp_ref[:, cols]
        o_ref[:, cols] = y.astype(BF16) * _silu_bf16(zp_ref[:, cols])


def _pool_gate_call(p, w_pool, s_pool_l, layer, seq, t):
    m = p.shape[0]
    tiles_per_seq = seq // t
    hpt = t // POOL_HALO
    n_halo_blocks = m // POOL_HALO
    wp_shape = w_pool.shape[1:]
    bands = jnp.asarray(_pool_bands(), BF16)
    blk = 2 * (3 * t * D_POOL * 2 + 2 * POOL_HALO * D_POOL * 2 + int(np.prod(wp_shape)) * 4 + D_POOL * 4
               + bands.size * 2)
    tmp = 8 * t * D_POOL_GROUP * 4
    kern = functools.partial(_pool_gate_kernel, tiles_per_seq=tiles_per_seq, seq=seq)
    return pl.pallas_call(
        kern,
        grid=(m // t,),
        in_specs=[
            pl.BlockSpec((t, D_POOL), lambda i: (i, COL_U // D_POOL)),
            pl.BlockSpec((POOL_HALO, D_POOL), lambda i: (jnp.maximum(i * hpt - 1, 0), COL_U // D_POOL)),
            pl.BlockSpec((POOL_HALO, D_POOL),
                         lambda i: (jnp.minimum((i + 1) * hpt, n_halo_blocks - 1), COL_U // D_POOL)),
            pl.BlockSpec((t, D_POOL), lambda i: (i, COL_ZP // D_POOL)),
            pl.BlockSpec(bands.shape, lambda i: (0, 0, 0)),
            pl.BlockSpec((None,) + wp_shape, lambda i: (layer, 0, 0, 0)),
            pl.BlockSpec((1, D_POOL), lambda i: (0, 0)),
        ],
        out_specs=pl.BlockSpec((t, D_POOL), lambda i: (i, 0)),
        out_shape=jax.ShapeDtypeStruct((m, D_POOL), BF16),
        compiler_params=_params(blk + tmp, 1),
        name="pool_gate",
    )(p, p, p, p, bands, w_pool, s_pool_l.reshape(1, D_POOL))


def _sigmoid_bf16(z):
    return 0.5 + 0.5 * jnp.tanh(z * 0.5)


def _merge_kernel(ap_ref, aa_ref, wp_ref, wa_ref, gp_ref, ga_ref, o_ref):
    br_p = jnp.dot(ap_ref[...], wp_ref[...], preferred_element_type=F32)
    br_a = jnp.dot(aa_ref[...], wa_ref[...], preferred_element_type=F32)
    gp = _sigmoid_bf16(gp_ref[...]).astype(F32)
    ga = _sigmoid_bf16(ga_ref[...]).astype(F32)
    o_ref[...] = (gp * br_p + ga * br_a).astype(o_ref.dtype)


def _merge_call(a_pool, a_attn, w_br_pool_b, w_br_attn_b, p, tm, tn):
    m = a_pool.shape[0]
    n = D_MODEL
    blk = 2 * (tm * D_MODEL * 2 + 2 * D_POOL * tn * 2 + 3 * tm * tn * 2)
    tmp = 2 * tm * tn * 4
    return pl.pallas_call(
        _merge_kernel,
        grid=(m // tm, n // tn),
        in_specs=[
            pl.BlockSpec((tm, D_POOL), lambda i, j: (i, 0)),
            pl.BlockSpec((tm, D_ATTN), lambda i, j: (i, 0)),
            pl.BlockSpec((D_POOL, tn), lambda i, j: (0, j)),
            pl.BlockSpec((D_ATTN, tn), lambda i, j: (0, j)),
            pl.BlockSpec((tm, tn), lambda i, j: (i, COL_GP // tn + j)),
            pl.BlockSpec((tm, tn), lambda i, j: (i, COL_GA // tn + j)),
        ],
        out_specs=pl.BlockSpec((tm, tn), lambda i, j: (i, j)),
        out_shape=jax.ShapeDtypeStruct((m, n), BF16),
        compiler_params=_params(blk + tmp, 2),
        name="merge",
    )(a_pool, a_attn, w_br_pool_b, w_br_attn_b, p, p)


def _out_kernel(m_ref, w_ref, x_ref, gt_ref, o_ref):
    acc = jnp.dot(m_ref[...], w_ref[...], preferred_element_type=F32)
    o_ref[...] = x_ref[...] + gt_ref[0] * acc


def _out_call(mix, w_out_b, x2, gt, row_of_tile, tm, tn):
    m, k = mix.shape
    n = w_out_b.shape[1]
    blk = 2 * (tm * k * 2 + k * tn * 2 + 2 * tm * tn * 4 + tn * 4)
    tmp = tm * tn * 4
    return pl.pallas_call(
        _out_kernel,
        grid=(m // tm, n // tn),
        in_specs=[
            pl.BlockSpec((tm, k), lambda i, j: (i, 0)),
            pl.BlockSpec((k, tn), lambda i, j: (0, j)),
            pl.BlockSpec((tm, tn), lambda i, j: (i, j)),
            pl.BlockSpec((1, 1, tn), lambda i, j: (row_of_tile(i), 0, j)),
        ],
        out_specs=pl.BlockSpec((tm, tn), lambda i, j: (i, j)),
        out_shape=jax.ShapeDtypeStruct((m, n), F32),
        compiler_params=_params(blk + tmp, 2),
        name="out_proj",
    )(mix, w_out_b, x2, gt)


def kernel(x, c, ctx, c_ctx, norm_g, w_ada, b_ada, w_in, b_in, w_pool, s_pool, rpb,
           w_br_pool, w_br_attn, w_out, final_g):
    batch, seq, d = x.shape
    ctx_len = ctx.shape[1]
    depth = w_in.shape[0]
    ctx_row = batch

    x_lat = x.reshape(batch * seq, d)
    x_ctx = ctx.reshape(batch * ctx_len, d)
    cvec = jnp.zeros((8, d), F32).at[:batch].set(c).at[ctx_row].set(c_ctx)
    ada = _ada_call(cvec, w_ada, b_ada)
    c2 = _bias_pair_table(rpb)

    tm_in, tn_in = 2048, 512
    tm_out, tn_out = 1024, 1024
    tm_mrg, tn_mrg = 1024, 512
    tm_ctx = batch * ctx_len
    tn_ctx = 1024
    t_row = 256
    t_norm = 512
    lat_row = lambda tile_rows: (lambda i: i // (seq // tile_rows))
    ctx_row_fn = lambda i: ctx_row

    for l in range(depth):
        last = l == depth - 1
        mod = ada[l].reshape(8, 3, 1, d)
        sh, sc, gt = mod[:, 0], mod[:, 1], mod[:, 2]

        h_lat = _mod_call(x_lat, norm_g[l], sh, sc, lat_row(t_norm), t_norm)
        h_ctx = _mod_call(x_ctx, norm_g[l], sh, sc, ctx_row_fn, t_row)
        p_lat = _proj_call(h_lat, w_in, b_in, l, 0, D_IN, tm_in, tn_in)
        if last:
            p_ctx = _proj_call(h_ctx, w_in, b_in, l, COL_K, 2 * D_ATTN, tm_ctx, tn_in)
            kc_col, vc_col = 0, D_ATTN
        else:
            p_ctx = _proj_call(h_ctx, w_in, b_in, l, 0, D_IN, tm_ctx, tn_ctx)
            kc_col, vc_col = COL_K, COL_V

        a_attn, w_brp_b, w_bra_b, w_out_b = _na_call(p_lat, p_ctx, c2, w_br_pool, w_br_attn, w_out, l,
                                                     batch, seq, ctx_len, kc_col, vc_col)
        a_pool = _pool_gate_call(p_lat, w_pool, s_pool[l], l, seq, t_row)
        mix = _merge_call(a_pool, a_attn, w_brp_b, w_bra_b, p_lat, tm_mrg, tn_mrg)
        x_lat_new = _out_call(mix, w_out_b, x_lat, gt, lat_row(tm_out), tm_out, tn_out)

        if not last:
            a_attn_c = _ctx_attn_call(p_ctx, batch, ctx_len)
            a_pool_c = _pool_gate_call(p_ctx, w_pool, s_pool[l], l, ctx_len, t_row)
            mix_c = _merge_call(a_pool_c, a_attn_c, w_brp_b, w_bra_b, p_ctx, tm_ctx, tn_mrg)
            x_ctx = _out_call(mix_c, w_out_b, x_ctx, gt, ctx_row_fn, tm_ctx, tn_out)
        x_lat = x_lat_new

    return _rms_call(x_lat, final_g, t_norm).reshape(batch, seq, d)
```

```python
import functools

import numpy as np
import jax
import jax.numpy as jnp
from jax import lax
from jax.experimental import pallas as pl
from jax.experimental.pallas import tpu as pltpu

F32 = jnp.float32
BF16 = jnp.bfloat16

D_MODEL = 4096
GRID_W = 64
LOG2_GRID_W = 6
D_POOL = D_MODEL // 2
POOL_WINDOWS = (2, 4, 8, 16)
D_POOL_GROUP = D_POOL // len(POOL_WINDOWS)
HEAD_DIM = 128
D_ATTN = D_MODEL // 2
N_HEADS = D_ATTN // HEAD_DIM
NA_KH = 8
NA_KW = 16
D_IN = 2 * D_POOL + 4 * D_ATTN + 2 * D_MODEL
RMS_EPS = 1e-6
NEG_INF = -1e30
ATTN_SCALE = HEAD_DIM ** -0.5
LOG2E = 1.4426950408889634

COL_U = 0
COL_ZP = D_POOL
COL_Q = 2 * D_POOL
COL_K = COL_Q + D_ATTN
COL_V = COL_K + D_ATTN
COL_ZA = COL_V + D_ATTN
COL_GP = COL_ZA + D_ATTN
COL_GA = COL_GP + D_MODEL

V7X_VMEM_LIMIT_BYTES = 60000 * 1024
COMPILER_SCRATCH_BYTES = 16 << 20
MATMUL_SPILL_BYTES = 4 << 20

QROWS = 4
BAND_ROWS = 12
TQ = QROWS * GRID_W
TK = BAND_ROWS * GRID_W
HEADS_PER_STEP = 4
CAST_ROWS, CAST_COLS = 512, 1024
POOL_HALO = 64
POOL_SUBTILE = 128


def _params(block_bytes, n_axes):
    return pltpu.CompilerParams(
        dimension_semantics=("arbitrary",) * n_axes,
        vmem_limit_bytes=int(min(V7X_VMEM_LIMIT_BYTES, block_bytes + COMPILER_SCRATCH_BYTES)),
    )


def _sigmoid(x):
    return 1.0 / (1.0 + jnp.exp(-x))


def _silu_bf16(z):
    hz = z * 0.5
    return hz + hz * jnp.tanh(hz)


def _split_bf16(v):
    hi = v.astype(BF16)
    lo = (v - hi.astype(F32)).astype(BF16)
    return hi, lo


def _ada_kernel(c_ref, w_ref, b_ref, o_ref):
    cv = c_ref[...]
    s_hi, s_lo = _split_bf16(cv * _sigmoid(cv))
    w_hi, w_lo = _split_bf16(w_ref[...])
    rows = s_hi.shape[0]
    r_hi = jnp.dot(jnp.concatenate([s_hi, s_lo], axis=0), w_hi, preferred_element_type=F32)
    r_lo = jnp.dot(s_hi, w_lo, preferred_element_type=F32)
    o_ref[...] = r_hi[:rows] + r_hi[rows:] + r_lo + b_ref[...]


def _ada_call(cvec, w_ada, b_ada):
    depth, d, n = w_ada.shape
    tn = 512
    blk = 2 * (d * tn * 4) + 2 * 8 * d * 4 + 4 * 8 * tn * 4
    return pl.pallas_call(
        _ada_kernel,
        grid=(depth, n // tn),
        in_specs=[
            pl.BlockSpec((8, d), lambda l, j: (0, 0)),
            pl.BlockSpec((None, d, tn), lambda l, j: (l, 0, j)),
            pl.BlockSpec((None, 1, tn), lambda l, j: (l, 0, j)),
        ],
        out_specs=pl.BlockSpec((None, 8, tn), lambda l, j: (l, 0, j)),
        out_shape=jax.ShapeDtypeStruct((depth, 8, n), F32),
        compiler_params=_params(blk, 2),
        name="ada",
    )(cvec, w_ada, b_ada.reshape(depth, 1, n))


def _mod_kernel(x_ref, g_ref, sh_ref, sc_ref, o_ref):
    x = x_ref[...]
    ms = jnp.mean(x * x, axis=-1, keepdims=True)
    y = x * lax.rsqrt(ms + RMS_EPS) * g_ref[...]
    o_ref[...] = (y * (1.0 + sc_ref[0]) + sh_ref[0]).astype(o_ref.dtype)


def _mod_call(x2, g, sh, sc, row_of_tile, tr):
    m, d = x2.shape
    vec = pl.BlockSpec((1, 1, d), lambda i: (row_of_tile(i), 0, 0))
    return pl.pallas_call(
        _mod_kernel,
        grid=(m // tr,),
        in_specs=[
            pl.BlockSpec((tr, d), lambda i: (i, 0)),
            pl.BlockSpec((1, d), lambda i: (0, 0)),
            vec, vec,
        ],
        out_specs=pl.BlockSpec((tr, d), lambda i: (i, 0)),
        out_shape=jax.ShapeDtypeStruct((m, d), BF16),
        compiler_params=_params(2 * tr * d * 6 + 3 * tr * d * 4, 1),
        name="modulate",
    )(x2, g.reshape(1, d), sh, sc)


def _rms_kernel(x_ref, g_ref, o_ref):
    x = x_ref[...]
    ms = jnp.mean(x * x, axis=-1, keepdims=True)
    o_ref[...] = x * lax.rsqrt(ms + RMS_EPS) * g_ref[...]


def _rms_call(x2, g, tr):
    m, d = x2.shape
    return pl.pallas_call(
        _rms_kernel,
        grid=(m // tr,),
        in_specs=[pl.BlockSpec((tr, d), lambda i: (i, 0)),
                  pl.BlockSpec((1, d), lambda i: (0, 0))],
        out_specs=pl.BlockSpec((tr, d), lambda i: (i, 0)),
        out_shape=jax.ShapeDtypeStruct((m, d), F32),
        compiler_params=_params(2 * tr * d * 8 + 2 * tr * d * 4, 1),
        name="final_norm",
    )(x2, g.reshape(1, d))


def _proj_kernel(a_ref, w_ref, b_ref, o_ref):
    acc = jnp.dot(a_ref[...], w_ref[...].astype(BF16), preferred_element_type=F32)
    o_ref[...] = (acc + b_ref[...]).astype(o_ref.dtype)


def _proj_call(a, w_stack, b_stack, layer, col0, n, tm, tn):
    m, k = a.shape
    depth, _, n_all = w_stack.shape
    jb = col0 // tn
    rows = tm * k * 2
    rest = 2 * (k * tn * 4 + tm * tn * 2 + tn * 4) + MATMUL_SPILL_BYTES
    row_buffers = 2 if 2 * rows + rest <= V7X_VMEM_LIMIT_BYTES else 1
    blk = row_buffers * rows + rest
    tmp = 0
    return pl.pallas_call(
        _proj_kernel,
        grid=(m // tm, n // tn),
        in_specs=[
            pl.BlockSpec((tm, k), lambda i, j: (i, 0), pipeline_mode=pl.Buffered(row_buffers)),
            pl.BlockSpec((None, k, tn), lambda i, j: (layer, 0, jb + j)),
            pl.BlockSpec((None, 1, tn), lambda i, j: (layer, 0, jb + j)),
        ],
        out_specs=pl.BlockSpec((tm, tn), lambda i, j: (i, j)),
        out_shape=jax.ShapeDtypeStruct((m, n), BF16),
        compiler_params=_params(blk + tmp, 2),
        name="in_proj",
    )(a, w_stack, b_stack.reshape(depth, 1, n_all))


def _dot_nt(a, b):
    return lax.dot_general(a, b, (((1,), (1,)), ((), ())), preferred_element_type=F32)


def _scaled_q(q_bf16):
    return (q_bf16.astype(F32) * (ATTN_SCALE * LOG2E)).astype(BF16)


def _with_ones(v):
    return jnp.concatenate([v, jnp.ones_like(v)], axis=1)


def _softmax_pv(scores, values, out_dtype):
    mx = functools.reduce(jnp.maximum, [jnp.max(s, axis=-1, keepdims=True) for s in scores])
    acc = functools.reduce(jnp.add, [jnp.dot(jnp.exp2(s - mx).astype(BF16), _with_ones(v), preferred_element_type=F32)
                                     for s, v in zip(scores, values)])
    return (acc[:, :HEAD_DIM] / acc[:, HEAD_DIM:]).astype(out_dtype)


def _band_start(g, n_rows):
    kr0 = jnp.clip(QROWS * g - NA_KH // 2, 0, n_rows - BAND_ROWS)
    return pl.multiple_of(kr0 * GRID_W, GRID_W)


def _na_scores(q_ref, k_ref, kc_ref, tab_ref, dst_ref, start):
    for hh in range(HEADS_PER_STEP):
        lanes = slice(hh * HEAD_DIM, (hh + 1) * HEAD_DIM)
        q = _scaled_q(q_ref[:, lanes])
        dst_ref[hh, :, :TK] = _dot_nt(q, k_ref[pl.ds(start, TK), lanes]) + tab_ref[hh]
        dst_ref[hh, :, TK:] = _dot_nt(q, kc_ref[:, lanes])


def _cast_tile_index(t, n_col_blocks, n_tiles):
    k = jnp.minimum(t, n_tiles - 1)
    return k // n_col_blocks, k % n_col_blocks


def _na_kernel(q_ref, k_ref, v_ref, kc_ref, vc_ref, za_ref, c2_ref, w_ref, o_ref, w_out_ref,
               tab_ref, sa_ref, sb_ref, *, n_groups, n_rows, n_cast_tiles):
    step_id = pl.program_id(2)
    t_lin = (pl.program_id(0) * pl.num_programs(1) + pl.program_id(1)) * pl.num_programs(2) + step_id

    @pl.when(t_lin < n_cast_tiles)
    def _():
        w_out_ref[...] = w_ref[...].astype(w_out_ref.dtype)

    last = n_groups - 1
    g = jnp.minimum(step_id, last)

    @pl.when(jnp.logical_or(g <= 1, g == last))
    def _():
        interior = jnp.logical_and(g > 0, g < last)
        lo_a = jnp.where(g == last, BAND_ROWS - NA_KH, 0)
        lo_b = jnp.where(interior, 1, 0)
        off = jnp.where(g == 0, NA_KH - 1,
                        jnp.where(g == last, NA_KH - 1 - BAND_ROWS + QROWS, NA_KH - 1 - NA_KH // 2))
        qrow = jnp.right_shift(lax.broadcasted_iota(jnp.int32, (TQ, 1), 0), LOG2_GRID_W)
        lo = lo_a + lo_b * qrow
        jrow = jnp.right_shift(lax.broadcasted_iota(jnp.int32, (1, TK), 1), LOG2_GRID_W)
        row_ok = jnp.logical_and(jrow >= lo, jrow < lo + NA_KH)
        for hh in range(HEADS_PER_STEP):
            bias = jnp.concatenate(
                [jnp.concatenate([c2_ref[hh, 2 * m - i + off + QROWS] for m in range(BAND_ROWS // 2)], axis=1)
                 for i in range(QROWS)], axis=0)
            tab_ref[hh] = jnp.where(row_ok, bias * LOG2E, NEG_INF)

    @pl.when(step_id == 0)
    def _():
        _na_scores(q_ref, k_ref, kc_ref, tab_ref, sb_ref, _band_start(0, n_rows))

    kstart = _band_start(g, n_rows)
    vstart = _band_start(jnp.maximum(step_id - 1, 0), n_rows)

    def step(src_ref, dst_ref):
        _na_scores(q_ref, k_ref, kc_ref, tab_ref, dst_ref, kstart)
        for hh in range(HEADS_PER_STEP):
            lanes = slice(hh * HEAD_DIM, (hh + 1) * HEAD_DIM)
            y = _softmax_pv((src_ref[hh, :, :TK], src_ref[hh, :, TK:]),
                            (v_ref[pl.ds(vstart, TK), lanes], vc_ref[:, lanes]), o_ref.dtype)
            o_ref[:, lanes] = y * _silu_bf16(za_ref[:, lanes])

    @pl.when(step_id % 2 == 0)
    def _():
        step(sb_ref, sa_ref)

    @pl.when(step_id % 2 == 1)
    def _():
        step(sa_ref, sb_ref)


def _na_call(p, pc, c2, w_out, layer, batch, seq, ctx_len, kc_col, vc_col):
    n_rows = seq // GRID_W
    n_groups = n_rows // QROWS
    last = n_groups - 1
    wb = HEADS_PER_STEP * HEAD_DIM
    n_c2 = c2.shape[2]
    c2_bytes = HEADS_PER_STEP * n_c2 * GRID_W * 2 * GRID_W * 4
    tab_bytes = HEADS_PER_STEP * TQ * TK * 4
    score_bytes = HEADS_PER_STEP * TQ * (TK + ctx_len) * 4
    blk = 2 * (TQ * wb * 2 * 3 + 2 * seq * wb * 2 + 2 * ctx_len * wb * 2 + c2_bytes
               + CAST_ROWS * CAST_COLS * 6)
    n_hg = N_HEADS // HEADS_PER_STEP
    steps = n_groups + 1
    n_cb = w_out.shape[2] // CAST_COLS
    n_cast_tiles = (w_out.shape[1] // CAST_ROWS) * n_cb
    assert n_cast_tiles <= batch * n_hg * steps

    def cast_tile(b, h, s):
        return _cast_tile_index((b * n_hg + h) * steps + s, n_cb, n_cast_tiles)

    kern = functools.partial(_na_kernel, n_groups=n_groups, n_rows=n_rows, n_cast_tiles=n_cast_tiles)
    return pl.pallas_call(
        kern,
        grid=(batch, N_HEADS // HEADS_PER_STEP, n_groups + 1),
        in_specs=[
            pl.BlockSpec((TQ, wb), lambda b, h, s: (b * n_groups + jnp.minimum(s, last), COL_Q // wb + h)),
            pl.BlockSpec((seq, wb), lambda b, h, s: (b, COL_K // wb + h)),
            pl.BlockSpec((seq, wb), lambda b, h, s: (b, COL_V // wb + h)),
            pl.BlockSpec((ctx_len, wb), lambda b, h, s: (b, kc_col // wb + h)),
            pl.BlockSpec((ctx_len, wb), lambda b, h, s: (b, vc_col // wb + h)),
            pl.BlockSpec((TQ, wb), lambda b, h, s: (b * n_groups + jnp.maximum(s - 1, 0), COL_ZA // wb + h)),
            pl.BlockSpec((None, HEADS_PER_STEP, n_c2, GRID_W, 2 * GRID_W), lambda b, h, s: (layer, h, 0, 0, 0)),
            pl.BlockSpec((None, CAST_ROWS, CAST_COLS), lambda b, h, s: (layer,) + cast_tile(b, h, s)),
        ],
        out_specs=[pl.BlockSpec((TQ, wb), lambda b, h, s: (b * n_groups + jnp.maximum(s - 1, 0), h)),
                   pl.BlockSpec((CAST_ROWS, CAST_COLS), cast_tile)],
        out_shape=[jax.ShapeDtypeStruct((batch * seq, D_ATTN), BF16),
                   jax.ShapeDtypeStruct(w_out.shape[1:], BF16)],
        scratch_shapes=[pltpu.VMEM((HEADS_PER_STEP, TQ, TK), F32),
                        pltpu.VMEM((HEADS_PER_STEP, TQ, TK + ctx_len), F32),
                        pltpu.VMEM((HEADS_PER_STEP, TQ, TK + ctx_len), F32)],
        compiler_params=_params(blk + tab_bytes + 2 * score_bytes, 3),
        name="na_attn",
    )(p, p, p, pc, pc, p, c2, w_out)


def _ctx_attn_kernel(q_ref, k_ref, v_ref, za_ref, o_ref):
    for hh in range(HEADS_PER_STEP):
        lanes = slice(hh * HEAD_DIM, (hh + 1) * HEAD_DIM)
        s = _dot_nt(_scaled_q(q_ref[:, lanes]), k_ref[:, lanes])
        o_ref[:, lanes] = _softmax_pv((s,), (v_ref[:, lanes],), o_ref.dtype) * _silu_bf16(za_ref[:, lanes])


def _ctx_attn_call(pc, batch, ctx_len):
    hb = HEADS_PER_STEP * HEAD_DIM
    return pl.pallas_call(
        _ctx_attn_kernel,
        grid=(batch, N_HEADS // HEADS_PER_STEP),
        in_specs=[
            pl.BlockSpec((ctx_len, hb), lambda b, h: (b, COL_Q // hb + h)),
            pl.BlockSpec((ctx_len, hb), lambda b, h: (b, COL_K // hb + h)),
            pl.BlockSpec((ctx_len, hb), lambda b, h: (b, COL_V // hb + h)),
            pl.BlockSpec((ctx_len, hb), lambda b, h: (b, COL_ZA // hb + h)),
        ],
        out_specs=pl.BlockSpec((ctx_len, hb), lambda b, h: (b, h)),
        out_shape=jax.ShapeDtypeStruct((batch * ctx_len, D_ATTN), BF16),
        compiler_params=_params(8 * ctx_len * hb * 2 + 8 * ctx_len * ctx_len * 4, 2),
        name="ctx_attn",
    )(pc, pc, pc, pc)


def _bias_pair_table(rpb):
    qc = np.arange(GRID_W)[:, None]
    kc = np.arange(GRID_W)[None, :]
    ws = np.clip(qc - NA_KW // 2, 0, GRID_W - NA_KW)
    col_ok = (kc >= ws) & (kc < ws + NA_KW)
    dc = kc - qc + NA_KW - 1
    n_dr, n_dc = 2 * NA_KH - 1, 2 * NA_KW - 1
    onehot = ((dc[None] == np.arange(n_dc)[:, None, None]) & col_ok[None]).astype(np.float32)
    n_e = BAND_ROWS + NA_KH - 1 + QROWS - 1
    pair_rows = np.stack([np.clip(np.arange(n_e) - QROWS, 0, n_dr - 1),
                          np.clip(np.arange(n_e) - QROWS + 1, 0, n_dr - 1)], axis=1)
    n_l, n_h = rpb.shape[:2]
    lhs = jnp.concatenate([rpb[:, :, pair_rows, :].reshape(n_l * n_h * n_e, 2 * n_dc),
                           jnp.ones((n_l * n_h * n_e, 1), F32)], axis=1)
    maps = np.zeros((2, n_dc, GRID_W, 2 * GRID_W), np.float32)
    maps[0, :, :, :GRID_W] = onehot
    maps[1, :, :, GRID_W:] = onehot
    mask_row = np.tile(np.where(col_ok, 0.0, NEG_INF).astype(np.float32), (1, 2))
    rhs = np.concatenate([maps.reshape(2 * n_dc, -1), mask_row.reshape(1, -1)], axis=0)
    table = jnp.dot(lhs, jnp.asarray(rhs), precision=lax.Precision.HIGHEST)
    return table.reshape(n_l, n_h, n_e, GRID_W, 2 * GRID_W)


def _pool_bands():
    r = np.arange(POOL_SUBTILE)[:, None]
    c = np.arange(POOL_SUBTILE + 2 * POOL_HALO)[None, :] - POOL_HALO
    return np.stack([(c >= r - w // 2) & (c <= r - w // 2 + w - 1) for w in POOL_WINDOWS]).astype(np.float32)


def _pool_gate_kernel(ucur_ref, uprev_ref, unext_ref, zp_ref, band_ref, wp_ref, sp_ref, o_ref,
                      *, tiles_per_seq, seq):
    t = ucur_ref.shape[0]
    cg = D_POOL_GROUP
    ts = pl.program_id(0) % tiles_per_seq
    base = ts * t
    uprev = jnp.where(ts > 0, uprev_ref[...], jnp.zeros_like(uprev_ref))
    unext = jnp.where(ts < tiles_per_seq - 1, unext_ref[...], jnp.zeros_like(unext_ref))
    ucat = jnp.concatenate([uprev, ucur_ref[...], unext], axis=0)
    tpos = lax.broadcasted_iota(jnp.int32, (t, 1), 0) + base
    for gi, w in enumerate(POOL_WINDOWS):
        cols = slice(gi * cg, (gi + 1) * cg)
        wsum = jnp.concatenate(
            [jnp.dot(band_ref[gi], ucat[r0:r0 + POOL_SUBTILE + 2 * POOL_HALO, cols], preferred_element_type=F32)
             for r0 in range(0, t, POOL_SUBTILE)], axis=0)
        cnt = (jnp.minimum(tpos - w // 2 + w - 1, seq - 1) - jnp.maximum(tpos - w // 2, 0) + 1).astype(F32)
        pooled = wsum / cnt - ucur_ref[:, cols].astype(F32)
        y = jnp.dot(pooled.astype(BF16), wp_ref[gi].astype(BF16), preferred_element_type=F32) * sp_ref[:, cols]
        o_ref[:, cols] = y.astype(BF16) * _silu_bf16(zp_ref[:, cols])


def _pool_gate_call(p, w_pool, s_pool_l, layer, seq, t):
    m = p.shape[0]
    tiles_per_seq = seq // t
    hpt = t // POOL_HALO
    n_halo_blocks = m // POOL_HALO
    wp_shape = w_pool.shape[1:]
    bands = jnp.asarray(_pool_bands(), BF16)
    blk = 2 * (3 * t * D_POOL * 2 + 2 * POOL_HALO * D_POOL * 2 + int(np.prod(wp_shape)) * 4 + D_POOL * 4
               + bands.size * 2)
    tmp = 8 * t * D_POOL_GROUP * 4
    kern = functools.partial(_pool_gate_kernel, tiles_per_seq=tiles_per_seq, seq=seq)
    return pl.pallas_call(
        kern,
        grid=(m // t,),
        in_specs=[
            pl.BlockSpec((t, D_POOL), lambda i: (i, COL_U // D_POOL)),
            pl.BlockSpec((POOL_HALO, D_POOL), lambda i: (jnp.maximum(i * hpt - 1, 0), COL_U // D_POOL)),
            pl.BlockSpec((POOL_HALO, D_POOL),
                         lambda i: (jnp.minimum((i + 1) * hpt, n_halo_blocks - 1), COL_U // D_POOL)),
            pl.BlockSpec((t, D_POOL), lambda i: (i, COL_ZP // D_POOL)),
            pl.BlockSpec(bands.shape, lambda i: (0, 0, 0)),
            pl.BlockSpec((None,) + wp_shape, lambda i: (layer, 0, 0, 0)),
            pl.BlockSpec((1, D_POOL), lambda i: (0, 0)),
        ],
        out_specs=pl.BlockSpec((t, D_POOL), lambda i: (i, 0)),
        out_shape=jax.ShapeDtypeStruct((m, D_POOL), BF16),
        compiler_params=_params(blk + tmp, 1),
        name="pool_gate",
    )(p, p, p, p, bands, w_pool, s_pool_l.reshape(1, D_POOL))


def _sigmoid_bf16(z):
    return 0.5 + 0.5 * jnp.tanh(z * 0.5)


def _merge_kernel(ap_ref, aa_ref, wp_ref, wa_ref, gp_ref, ga_ref, o_ref):
    br_p = jnp.dot(ap_ref[...], wp_ref[...].astype(BF16), preferred_element_type=F32)
    br_a = jnp.dot(aa_ref[...], wa_ref[...].astype(BF16), preferred_element_type=F32)
    gp = _sigmoid_bf16(gp_ref[...]).astype(F32)
    ga = _sigmoid_bf16(ga_ref[...]).astype(F32)
    o_ref[...] = (gp * br_p + ga * br_a).astype(o_ref.dtype)


def _merge_call(a_pool, a_attn, w_br_pool, w_br_attn, p, layer, tm, tn):
    m = a_pool.shape[0]
    n = D_MODEL
    blk = 2 * (tm * D_MODEL * 2 + 2 * D_POOL * tn * 4 + 3 * tm * tn * 2)
    tmp = 2 * tm * tn * 4
    return pl.pallas_call(
        _merge_kernel,
        grid=(m // tm, n // tn),
        in_specs=[
            pl.BlockSpec((tm, D_POOL), lambda i, j: (i, 0)),
            pl.BlockSpec((tm, D_ATTN), lambda i, j: (i, 0)),
            pl.BlockSpec((None, D_POOL, tn), lambda i, j: (layer, 0, j)),
            pl.BlockSpec((None, D_ATTN, tn), lambda i, j: (layer, 0, j)),
            pl.BlockSpec((tm, tn), lambda i, j: (i, COL_GP // tn + j)),
            pl.BlockSpec((tm, tn), lambda i, j: (i, COL_GA // tn + j)),
        ],
        out_specs=pl.BlockSpec((tm, tn), lambda i, j: (i, j)),
        out_shape=jax.ShapeDtypeStruct((m, n), BF16),
        compiler_params=_params(blk + tmp, 2),
        name="merge",
    )(a_pool, a_attn, w_br_pool, w_br_attn, p, p)


def _out_kernel(m_ref, w_ref, x_ref, gt_ref, o_ref):
    acc = jnp.dot(m_ref[...], w_ref[...], preferred_element_type=F32)
    o_ref[...] = x_ref[...] + gt_ref[0] * acc


def _out_call(mix, w_out_b, x2, gt, row_of_tile, tm, tn):
    m, k = mix.shape
    n = w_out_b.shape[1]
    blk = 2 * (tm * k * 2 + k * tn * 2 + 2 * tm * tn * 4 + tn * 4)
    tmp = tm * tn * 4
    return pl.pallas_call(
        _out_kernel,
        grid=(m // tm, n // tn),
        in_specs=[
            pl.BlockSpec((tm, k), lambda i, j: (i, 0)),
            pl.BlockSpec((k, tn), lambda i, j: (0, j)),
            pl.BlockSpec((tm, tn), lambda i, j: (i, j)),
            pl.BlockSpec((1, 1, tn), lambda i, j: (row_of_tile(i), 0, j)),
        ],
        out_specs=pl.BlockSpec((tm, tn), lambda i, j: (i, j)),
        out_shape=jax.ShapeDtypeStruct((m, n), F32),
        compiler_params=_params(blk + tmp, 2),
        name="out_proj",
    )(mix, w_out_b, x2, gt)


def kernel(x, c, ctx, c_ctx, norm_g, w_ada, b_ada, w_in, b_in, w_pool, s_pool, rpb,
           w_br_pool, w_br_attn, w_out, final_g):
    batch, seq, d = x.shape
    ctx_len = ctx.shape[1]
    depth = w_in.shape[0]
    ctx_row = batch

    x_lat = x.reshape(batch * seq, d)
    x_ctx = ctx.reshape(batch * ctx_len, d)
    cvec = jnp.zeros((8, d), F32).at[:batch].set(c).at[ctx_row].set(c_ctx)
    ada = _ada_call(cvec, w_ada, b_ada)
    c2 = _bias_pair_table(rpb)

    tm_in, tn_in = 2048, 512
    tm_out, tn_out = 1024, 1024
    tm_mrg, tn_mrg = 1024, 512
    tm_ctx = batch * ctx_len
    tn_ctx = 1024
    t_row = 256
    t_norm = 512
    lat_row = lambda tile_rows: (lambda i: i // (seq // tile_rows))
    ctx_row_fn = lambda i: ctx_row

    for l in range(depth):
        last = l == depth - 1
        mod = ada[l].reshape(8, 3, 1, d)
        sh, sc, gt = mod[:, 0], mod[:, 1], mod[:, 2]

        h_lat = _mod_call(x_lat, norm_g[l], sh, sc, lat_row(t_norm), t_norm)
        h_ctx = _mod_call(x_ctx, norm_g[l], sh, sc, ctx_row_fn, t_row)
        p_lat = _proj_call(h_lat, w_in, b_in, l, 0, D_IN, tm_in, tn_in)
        if last:
            p_ctx = _proj_call(h_ctx, w_in, b_in, l, COL_K, 2 * D_ATTN, tm_ctx, tn_in)
            kc_col, vc_col = 0, D_ATTN
        else:
            p_ctx = _proj_call(h_ctx, w_in, b_in, l, 0, D_IN, tm_ctx, tn_ctx)
            kc_col, vc_col = COL_K, COL_V

        a_attn, w_out_b = _na_call(p_lat, p_ctx, c2, w_out, l, batch, seq, ctx_len, kc_col, vc_col)
        a_pool = _pool_gate_call(p_lat, w_pool, s_pool[l], l, seq, t_row)
        mix = _merge_call(a_pool, a_attn, w_br_pool, w_br_attn, p_lat, l, tm_mrg, tn_mrg)
        x_lat_new = _out_call(mix, w_out_b, x_lat, gt, lat_row(tm_out), tm_out, tn_out)

        if not last:
            a_attn_c = _ctx_attn_call(p_ctx, batch, ctx_len)
            a_pool_c = _pool_gate_call(p_ctx, w_pool, s_pool[l], l, ctx_len, t_row)
            mix_c = _merge_call(a_pool_c, a_attn_c, w_br_pool, w_br_attn, p_ctx, l, tm_ctx, tn_mrg)
            x_ctx = _out_call(mix_c, w_out_b, x_ctx, gt, ctx_row_fn, tm_ctx, tn_out)
        x_lat = x_lat_new

    return _rms_call(x_lat, final_g, t_norm).reshape(batch, seq, d)
```

```python
import functools

import numpy as np
import jax
import jax.numpy as jnp
from jax import lax
from jax.experimental import pallas as pl
from jax.experimental.pallas import tpu as pltpu

F32 = jnp.float32
BF16 = jnp.bfloat16

D_MODEL = 4096
GRID_W = 64
LOG2_GRID_W = 6
D_POOL = D_MODEL // 2
POOL_WINDOWS = (2, 4, 8, 16)
D_POOL_GROUP = D_POOL // len(POOL_WINDOWS)
HEAD_DIM = 128
D_ATTN = D_MODEL // 2
N_HEADS = D_ATTN // HEAD_DIM
NA_KH = 8
NA_KW = 16
D_IN = 2 * D_POOL + 4 * D_ATTN + 2 * D_MODEL
RMS_EPS = 1e-6
NEG_INF = -1e30
ATTN_SCALE = HEAD_DIM ** -0.5
LOG2E = 1.4426950408889634

COL_U = 0
COL_ZP = D_POOL
COL_Q = 2 * D_POOL
COL_K = COL_Q + D_ATTN
COL_V = COL_K + D_ATTN
COL_ZA = COL_V + D_ATTN
COL_GP = COL_ZA + D_ATTN
COL_GA = COL_GP + D_MODEL

V7X_VMEM_LIMIT_BYTES = 60000 * 1024
COMPILER_SCRATCH_BYTES = 16 << 20
MATMUL_SPILL_BYTES = 4 << 20

QROWS = 4
BAND_ROWS = 12
TQ = QROWS * GRID_W
TK = BAND_ROWS * GRID_W
HEADS_PER_STEP = 4
CAST_ROWS, CAST_COLS = 128, 4096
POOL_HALO = 64
POOL_SUBTILE = 128


def _params(block_bytes, n_axes):
    return pltpu.CompilerParams(
        dimension_semantics=("arbitrary",) * n_axes,
        vmem_limit_bytes=int(min(V7X_VMEM_LIMIT_BYTES, block_bytes + COMPILER_SCRATCH_BYTES)),
    )


def _sigmoid(x):
    return 1.0 / (1.0 + jnp.exp(-x))


def _silu_bf16(z):
    hz = z * 0.5
    return hz + hz * jnp.tanh(hz)


def _split_bf16(v):
    hi = v.astype(BF16)
    lo = (v - hi.astype(F32)).astype(BF16)
    return hi, lo


def _ada_kernel(c_ref, w_ref, b_ref, o_ref):
    cv = c_ref[...]
    s_hi, s_lo = _split_bf16(cv * _sigmoid(cv))
    w_hi, w_lo = _split_bf16(w_ref[...])
    rows = s_hi.shape[0]
    r_hi = jnp.dot(jnp.concatenate([s_hi, s_lo], axis=0), w_hi, preferred_element_type=F32)
    r_lo = jnp.dot(s_hi, w_lo, preferred_element_type=F32)
    o_ref[...] = r_hi[:rows] + r_hi[rows:] + r_lo + b_ref[...]


def _ada_call(cvec, w_ada, b_ada):
    depth, d, n = w_ada.shape
    tn = 512
    blk = 2 * (d * tn * 4) + 2 * 8 * d * 4 + 4 * 8 * tn * 4
    return pl.pallas_call(
        _ada_kernel,
        grid=(depth, n // tn),
        in_specs=[
            pl.BlockSpec((8, d), lambda l, j: (0, 0)),
            pl.BlockSpec((None, d, tn), lambda l, j: (l, 0, j)),
            pl.BlockSpec((None, 1, tn), lambda l, j: (l, 0, j)),
        ],
        out_specs=pl.BlockSpec((None, 8, tn), lambda l, j: (l, 0, j)),
        out_shape=jax.ShapeDtypeStruct((depth, 8, n), F32),
        compiler_params=_params(blk, 2),
        name="ada",
    )(cvec, w_ada, b_ada.reshape(depth, 1, n))


def _mod_kernel(x_ref, g_ref, sh_ref, sc_ref, o_ref):
    x = x_ref[...]
    ms = jnp.mean(x * x, axis=-1, keepdims=True)
    y = x * lax.rsqrt(ms + RMS_EPS) * g_ref[...]
    o_ref[...] = (y * (1.0 + sc_ref[0]) + sh_ref[0]).astype(o_ref.dtype)


def _mod_call(x2, g, sh, sc, row_of_tile, tr):
    m, d = x2.shape
    vec = pl.BlockSpec((1, 1, d), lambda i: (row_of_tile(i), 0, 0))
    return pl.pallas_call(
        _mod_kernel,
        grid=(m // tr,),
        in_specs=[
            pl.BlockSpec((tr, d), lambda i: (i, 0)),
            pl.BlockSpec((1, d), lambda i: (0, 0)),
            vec, vec,
        ],
        out_specs=pl.BlockSpec((tr, d), lambda i: (i, 0)),
        out_shape=jax.ShapeDtypeStruct((m, d), BF16),
        compiler_params=_params(2 * tr * d * 6 + 3 * tr * d * 4, 1),
        name="modulate",
    )(x2, g.reshape(1, d), sh, sc)


def _rms_kernel(x_ref, g_ref, o_ref):
    x = x_ref[...]
    ms = jnp.mean(x * x, axis=-1, keepdims=True)
    o_ref[...] = x * lax.rsqrt(ms + RMS_EPS) * g_ref[...]


def _rms_call(x2, g, tr):
    m, d = x2.shape
    return pl.pallas_call(
        _rms_kernel,
        grid=(m // tr,),
        in_specs=[pl.BlockSpec((tr, d), lambda i: (i, 0)),
                  pl.BlockSpec((1, d), lambda i: (0, 0))],
        out_specs=pl.BlockSpec((tr, d), lambda i: (i, 0)),
        out_shape=jax.ShapeDtypeStruct((m, d), F32),
        compiler_params=_params(2 * tr * d * 8 + 2 * tr * d * 4, 1),
        name="final_norm",
    )(x2, g.reshape(1, d))


def _proj_kernel(a_ref, w_ref, b_ref, o_ref):
    acc = jnp.dot(a_ref[...], w_ref[...].astype(BF16), preferred_element_type=F32)
    o_ref[...] = (acc + b_ref[...]).astype(o_ref.dtype)


def _proj_call(a, w_stack, b_stack, layer, col0, n, tm, tn):
    m, k = a.shape
    depth, _, n_all = w_stack.shape
    jb = col0 // tn
    rows = tm * k * 2
    rest = 2 * (k * tn * 4 + tm * tn * 2 + tn * 4) + MATMUL_SPILL_BYTES
    row_buffers = 2 if 2 * rows + rest <= V7X_VMEM_LIMIT_BYTES else 1
    blk = row_buffers * rows + rest
    tmp = 0
    return pl.pallas_call(
        _proj_kernel,
        grid=(m // tm, n // tn),
        in_specs=[
            pl.BlockSpec((tm, k), lambda i, j: (i, 0), pipeline_mode=pl.Buffered(row_buffers)),
            pl.BlockSpec((None, k, tn), lambda i, j: (layer, 0, jb + j)),
            pl.BlockSpec((None, 1, tn), lambda i, j: (layer, 0, jb + j)),
        ],
        out_specs=pl.BlockSpec((tm, tn), lambda i, j: (i, j)),
        out_shape=jax.ShapeDtypeStruct((m, n), BF16),
        compiler_params=_params(blk + tmp, 2),
        name="in_proj",
    )(a, w_stack, b_stack.reshape(depth, 1, n_all))


def _dot_nt(a, b):
    return lax.dot_general(a, b, (((1,), (1,)), ((), ())), preferred_element_type=F32)


def _scaled_q(q_bf16):
    return (q_bf16.astype(F32) * (ATTN_SCALE * LOG2E)).astype(BF16)


def _with_ones(v):
    return jnp.concatenate([v, jnp.ones_like(v)], axis=1)


def _softmax_pv(scores, values, out_dtype):
    mx = functools.reduce(jnp.maximum, [jnp.max(s, axis=-1, keepdims=True) for s in scores])
    acc = functools.reduce(jnp.add, [jnp.dot(jnp.exp2(s - mx).astype(BF16), _with_ones(v), preferred_element_type=F32)
                                     for s, v in zip(scores, values)])
    return (acc[:, :HEAD_DIM] / acc[:, HEAD_DIM:]).astype(out_dtype)


def _band_start(g, n_rows):
    kr0 = jnp.clip(QROWS * g - NA_KH // 2, 0, n_rows - BAND_ROWS)
    return pl.multiple_of(kr0 * GRID_W, GRID_W)


def _na_scores(q_ref, k_ref, kc_ref, tab_ref, dst_ref, start):
    for hh in range(HEADS_PER_STEP):
        lanes = slice(hh * HEAD_DIM, (hh + 1) * HEAD_DIM)
        q = _scaled_q(q_ref[:, lanes])
        dst_ref[hh, :, :TK] = _dot_nt(q, k_ref[pl.ds(start, TK), lanes]) + tab_ref[hh]
        dst_ref[hh, :, TK:] = _dot_nt(q, kc_ref[:, lanes])


def _cast_tile_index(t, n_col_blocks, n_tiles):
    k = jnp.minimum(t, n_tiles - 1)
    return k // n_col_blocks, k % n_col_blocks


def _na_kernel(q_ref, k_ref, v_ref, kc_ref, vc_ref, za_ref, c2_ref, w_ref, o_ref, w_out_ref,
               tab_ref, sa_ref, sb_ref, *, n_groups, n_rows, n_cast_tiles):
    step_id = pl.program_id(2)
    t_lin = (pl.program_id(0) * pl.num_programs(1) + pl.program_id(1)) * pl.num_programs(2) + step_id

    @pl.when(t_lin < n_cast_tiles)
    def _():
        w_out_ref[...] = w_ref[...].astype(w_out_ref.dtype)

    last = n_groups - 1
    g = jnp.minimum(step_id, last)

    @pl.when(jnp.logical_or(g <= 1, g == last))
    def _():
        interior = jnp.logical_and(g > 0, g < last)
        lo_a = jnp.where(g == last, BAND_ROWS - NA_KH, 0)
        lo_b = jnp.where(interior, 1, 0)
        off = jnp.where(g == 0, NA_KH - 1,
                        jnp.where(g == last, NA_KH - 1 - BAND_ROWS + QROWS, NA_KH - 1 - NA_KH // 2))
        qrow = jnp.right_shift(lax.broadcasted_iota(jnp.int32, (TQ, 1), 0), LOG2_GRID_W)
        lo = lo_a + lo_b * qrow
        jrow = jnp.right_shift(lax.broadcasted_iota(jnp.int32, (1, TK), 1), LOG2_GRID_W)
        row_ok = jnp.logical_and(jrow >= lo, jrow < lo + NA_KH)
        for hh in range(HEADS_PER_STEP):
            bias = jnp.concatenate(
                [jnp.concatenate([c2_ref[hh, 2 * m - i + off + QROWS] for m in range(BAND_ROWS // 2)], axis=1)
                 for i in range(QROWS)], axis=0)
            tab_ref[hh] = jnp.where(row_ok, bias * LOG2E, NEG_INF)

    @pl.when(step_id == 0)
    def _():
        _na_scores(q_ref, k_ref, kc_ref, tab_ref, sb_ref, _band_start(0, n_rows))

    kstart = _band_start(g, n_rows)
    vstart = _band_start(jnp.maximum(step_id - 1, 0), n_rows)

    def step(src_ref, dst_ref):
        _na_scores(q_ref, k_ref, kc_ref, tab_ref, dst_ref, kstart)
        for hh in range(HEADS_PER_STEP):
            lanes = slice(hh * HEAD_DIM, (hh + 1) * HEAD_DIM)
            y = _softmax_pv((src_ref[hh, :, :TK], src_ref[hh, :, TK:]),
                            (v_ref[pl.ds(vstart, TK), lanes], vc_ref[:, lanes]), o_ref.dtype)
            o_ref[:, lanes] = y * _silu_bf16(za_ref[:, lanes])

    @pl.when(step_id % 2 == 0)
    def _():
        step(sb_ref, sa_ref)

    @pl.when(step_id % 2 == 1)
    def _():
        step(sa_ref, sb_ref)


def _na_call(p, pc, c2, w_out, layer, batch, seq, ctx_len, kc_col, vc_col):
    n_rows = seq // GRID_W
    n_groups = n_rows // QROWS
    last = n_groups - 1
    wb = HEADS_PER_STEP * HEAD_DIM
    n_c2 = c2.shape[2]
    c2_bytes = HEADS_PER_STEP * n_c2 * GRID_W * 2 * GRID_W * 4
    tab_bytes = HEADS_PER_STEP * TQ * TK * 4
    score_bytes = HEADS_PER_STEP * TQ * (TK + ctx_len) * 4
    blk = 2 * (TQ * wb * 2 * 3 + 2 * seq * wb * 2 + 2 * ctx_len * wb * 2 + c2_bytes
               + CAST_ROWS * CAST_COLS * 6)
    n_hg = N_HEADS // HEADS_PER_STEP
    steps = n_groups + 1
    n_cb = w_out.shape[2] // CAST_COLS
    n_cast_tiles = (w_out.shape[1] // CAST_ROWS) * n_cb
    assert n_cast_tiles <= batch * n_hg * steps

    def cast_tile(b, h, s):
        return _cast_tile_index((b * n_hg + h) * steps + s, n_cb, n_cast_tiles)

    kern = functools.partial(_na_kernel, n_groups=n_groups, n_rows=n_rows, n_cast_tiles=n_cast_tiles)
    return pl.pallas_call(
        kern,
        grid=(batch, N_HEADS // HEADS_PER_STEP, n_groups + 1),
        in_specs=[
            pl.BlockSpec((TQ, wb), lambda b, h, s: (b * n_groups + jnp.minimum(s, last), COL_Q // wb + h)),
            pl.BlockSpec((seq, wb), lambda b, h, s: (b, COL_K // wb + h)),
            pl.BlockSpec((seq, wb), lambda b, h, s: (b, COL_V // wb + h)),
            pl.BlockSpec((ctx_len, wb), lambda b, h, s: (b, kc_col // wb + h)),
            pl.BlockSpec((ctx_len, wb), lambda b, h, s: (b, vc_col // wb + h)),
            pl.BlockSpec((TQ, wb), lambda b, h, s: (b * n_groups + jnp.maximum(s - 1, 0), COL_ZA // wb + h)),
            pl.BlockSpec((None, HEADS_PER_STEP, n_c2, GRID_W, 2 * GRID_W), lambda b, h, s: (layer, h, 0, 0, 0)),
            pl.BlockSpec((None, CAST_ROWS, CAST_COLS), lambda b, h, s: (layer,) + cast_tile(b, h, s)),
        ],
        out_specs=[pl.BlockSpec((TQ, wb), lambda b, h, s: (b * n_groups + jnp.maximum(s - 1, 0), h)),
                   pl.BlockSpec((CAST_ROWS, CAST_COLS), cast_tile)],
        out_shape=[jax.ShapeDtypeStruct((batch * seq, D_ATTN), BF16),
                   jax.ShapeDtypeStruct(w_out.shape[1:], BF16)],
        scratch_shapes=[pltpu.VMEM((HEADS_PER_STEP, TQ, TK), F32),
                        pltpu.VMEM((HEADS_PER_STEP, TQ, TK + ctx_len), F32),
                        pltpu.VMEM((HEADS_PER_STEP, TQ, TK + ctx_len), F32)],
        compiler_params=_params(blk + tab_bytes + 2 * score_bytes, 3),
        name="na_attn",
    )(p, p, p, pc, pc, p, c2, w_out)


def _ctx_attn_kernel(q_ref, k_ref, v_ref, za_ref, o_ref):
    for hh in range(HEADS_PER_STEP):
        lanes = slice(hh * HEAD_DIM, (hh + 1) * HEAD_DIM)
        s = _dot_nt(_scaled_q(q_ref[:, lanes]), k_ref[:, lanes])
        o_ref[:, lanes] = _softmax_pv((s,), (v_ref[:, lanes],), o_ref.dtype) * _silu_bf16(za_ref[:, lanes])


def _ctx_attn_call(pc, batch, ctx_len):
    hb = HEADS_PER_STEP * HEAD_DIM
    return pl.pallas_call(
        _ctx_attn_kernel,
        grid=(batch, N_HEADS // HEADS_PER_STEP),
        in_specs=[
            pl.BlockSpec((ctx_len, hb), lambda b, h: (b, COL_Q // hb + h)),
            pl.BlockSpec((ctx_len, hb), lambda b, h: (b, COL_K // hb + h)),
            pl.BlockSpec((ctx_len, hb), lambda b, h: (b, COL_V // hb + h)),
            pl.BlockSpec((ctx_len, hb), lambda b, h: (b, COL_ZA // hb + h)),
        ],
        out_specs=pl.BlockSpec((ctx_len, hb), lambda b, h: (b, h)),
        out_shape=jax.ShapeDtypeStruct((batch * ctx_len, D_ATTN), BF16),
        compiler_params=_params(8 * ctx_len * hb * 2 + 8 * ctx_len * ctx_len * 4, 2),
        name="ctx_attn",
    )(pc, pc, pc, pc)


def _bias_pair_table(rpb):
    qc = np.arange(GRID_W)[:, None]
    kc = np.arange(GRID_W)[None, :]
    ws = np.clip(qc - NA_KW // 2, 0, GRID_W - NA_KW)
    col_ok = (kc >= ws) & (kc < ws + NA_KW)
    dc = kc - qc + NA_KW - 1
    n_dr, n_dc = 2 * NA_KH - 1, 2 * NA_KW - 1
    onehot = ((dc[None] == np.arange(n_dc)[:, None, None]) & col_ok[None]).astype(np.float32)
    n_e = BAND_ROWS + NA_KH - 1 + QROWS - 1
    pair_rows = np.stack([np.clip(np.arange(n_e) - QROWS, 0, n_dr - 1),
                          np.clip(np.arange(n_e) - QROWS + 1, 0, n_dr - 1)], axis=1)
    n_l, n_h = rpb.shape[:2]
    lhs = jnp.concatenate([rpb[:, :, pair_rows, :].reshape(n_l * n_h * n_e, 2 * n_dc),
                           jnp.ones((n_l * n_h * n_e, 1), F32)], axis=1)
    maps = np.zeros((2, n_dc, GRID_W, 2 * GRID_W), np.float32)
    maps[0, :, :, :GRID_W] = onehot
    maps[1, :, :, GRID_W:] = onehot
    mask_row = np.tile(np.where(col_ok, 0.0, NEG_INF).astype(np.float32), (1, 2))
    rhs = np.concatenate([maps.reshape(2 * n_dc, -1), mask_row.reshape(1, -1)], axis=0)
    table = jnp.dot(lhs, jnp.asarray(rhs), precision=lax.Precision.HIGHEST)
    return table.reshape(n_l, n_h, n_e, GRID_W, 2 * GRID_W)


def _pool_bands():
    r = np.arange(POOL_SUBTILE)[:, None]
    c = np.arange(POOL_SUBTILE + 2 * POOL_HALO)[None, :] - POOL_HALO
    return np.stack([(c >= r - w // 2) & (c <= r - w // 2 + w - 1) for w in POOL_WINDOWS]).astype(np.float32)


def _pool_gate_kernel(ucur_ref, uprev_ref, unext_ref, zp_ref, band_ref, wp_ref, sp_ref, o_ref,
                      *, tiles_per_seq, seq):
    t = ucur_ref.shape[0]
    cg = D_POOL_GROUP
    ts = pl.program_id(0) % tiles_per_seq
    base = ts * t
    uprev = jnp.where(ts > 0, uprev_ref[...], jnp.zeros_like(uprev_ref))
    unext = jnp.where(ts < tiles_per_seq - 1, unext_ref[...], jnp.zeros_like(unext_ref))
    ucat = jnp.concatenate([uprev, ucur_ref[...], unext], axis=0)
    tpos = lax.broadcasted_iota(jnp.int32, (t, 1), 0) + base
    for gi, w in enumerate(POOL_WINDOWS):
        cols = slice(gi * cg, (gi + 1) * cg)
        wsum = jnp.concatenate(
            [jnp.dot(band_ref[gi], ucat[r0:r0 + POOL_SUBTILE + 2 * POOL_HALO, cols], preferred_element_type=F32)
             for r0 in range(0, t, POOL_SUBTILE)], axis=0)
        cnt = (jnp.minimum(tpos - w // 2 + w - 1, seq - 1) - jnp.maximum(tpos - w // 2, 0) + 1).astype(F32)
        pooled = wsum / cnt - ucur_ref[:, cols].astype(F32)
        y = jnp.dot(pooled.astype(BF16), wp_ref[gi].astype(BF16), preferred_element_type=F32) * sp_ref[:, cols]
        o_ref[:, cols] = y.astype(BF16) * _silu_bf16(zp_ref[:, cols])


def _pool_gate_call(p, w_pool, s_pool_l, layer, seq, t):
    m = p.shape[0]
    tiles_per_seq = seq // t
    hpt = t // POOL_HALO
    n_halo_blocks = m // POOL_HALO
    wp_shape = w_pool.shape[1:]
    bands = jnp.asarray(_pool_bands(), BF16)
    blk = 2 * (3 * t * D_POOL * 2 + 2 * POOL_HALO * D_POOL * 2 + int(np.prod(wp_shape)) * 4 + D_POOL * 4
               + bands.size * 2)
    tmp = 8 * t * D_POOL_GROUP * 4
    kern = functools.partial(_pool_gate_kernel, tiles_per_seq=tiles_per_seq, seq=seq)
    return pl.pallas_call(
        kern,
        grid=(m // t,),
        in_specs=[
            pl.BlockSpec((t, D_POOL), lambda i: (i, COL_U // D_POOL)),
            pl.BlockSpec((POOL_HALO, D_POOL), lambda i: (jnp.maximum(i * hpt - 1, 0), COL_U // D_POOL)),
            pl.BlockSpec((POOL_HALO, D_POOL),
                         lambda i: (jnp.minimum((i + 1) * hpt, n_halo_blocks - 1), COL_U // D_POOL)),
            pl.BlockSpec((t, D_POOL), lambda i: (i, COL_ZP // D_POOL)),
            pl.BlockSpec(bands.shape, lambda i: (0, 0, 0)),
            pl.BlockSpec((None,) + wp_shape, lambda i: (layer, 0, 0, 0)),
            pl.BlockSpec((1, D_POOL), lambda i: (0, 0)),
        ],
        out_specs=pl.BlockSpec((t, D_POOL), lambda i: (i, 0)),
        out_shape=jax.ShapeDtypeStruct((m, D_POOL), BF16),
        compiler_params=_params(blk + tmp, 1),
        name="pool_gate",
    )(p, p, p, p, bands, w_pool, s_pool_l.reshape(1, D_POOL))


def _sigmoid_bf16(z):
    return 0.5 + 0.5 * jnp.tanh(z * 0.5)


def _merge_kernel(ap_ref, aa_ref, wp_ref, wa_ref, gp_ref, ga_ref, o_ref):
    br_p = jnp.dot(ap_ref[...], wp_ref[...].astype(BF16), preferred_element_type=F32)
    br_a = jnp.dot(aa_ref[...], wa_ref[...].astype(BF16), preferred_element_type=F32)
    gp = _sigmoid_bf16(gp_ref[...]).astype(F32)
    ga = _sigmoid_bf16(ga_ref[...]).astype(F32)
    o_ref[...] = (gp * br_p + ga * br_a).astype(o_ref.dtype)


def _merge_call(a_pool, a_attn, w_br_pool, w_br_attn, p, layer, tm, tn):
    m = a_pool.shape[0]
    n = D_MODEL
    blk = 2 * (tm * D_MODEL * 2 + 2 * D_POOL * tn * 4 + 3 * tm * tn * 2)
    tmp = 2 * tm * tn * 4
    return pl.pallas_call(
        _merge_kernel,
        grid=(m // tm, n // tn),
        in_specs=[
            pl.BlockSpec((tm, D_POOL), lambda i, j: (i, 0)),
            pl.BlockSpec((tm, D_ATTN), lambda i, j: (i, 0)),
            pl.BlockSpec((None, D_POOL, tn), lambda i, j: (layer, 0, j)),
            pl.BlockSpec((None, D_ATTN, tn), lambda i, j: (layer, 0, j)),
            pl.BlockSpec((tm, tn), lambda i, j: (i, COL_GP // tn + j)),
            pl.BlockSpec((tm, tn), lambda i, j: (i, COL_GA // tn + j)),
        ],
        out_specs=pl.BlockSpec((tm, tn), lambda i, j: (i, j)),
        out_shape=jax.ShapeDtypeStruct((m, n), BF16),
        compiler_params=_params(blk + tmp, 2),
        name="merge",
    )(a_pool, a_attn, w_br_pool, w_br_attn, p, p)


def _out_kernel(m_ref, w_ref, x_ref, gt_ref, o_ref):
    acc = jnp.dot(m_ref[...], w_ref[...], preferred_element_type=F32)
    o_ref[...] = x_ref[...] + gt_ref[0] * acc


def _out_call(mix, w_out_b, x2, gt, row_of_tile, tm, tn):
    m, k = mix.shape
    n = w_out_b.shape[1]
    blk = 2 * (tm * k * 2 + k * tn * 2 + 2 * tm * tn * 4 + tn * 4)
    tmp = tm * tn * 4
    return pl.pallas_call(
        _out_kernel,
        grid=(m // tm, n // tn),
        in_specs=[
            pl.BlockSpec((tm, k), lambda i, j: (i, 0)),
            pl.BlockSpec((k, tn), lambda i, j: (0, j)),
            pl.BlockSpec((tm, tn), lambda i, j: (i, j)),
            pl.BlockSpec((1, 1, tn), lambda i, j: (row_of_tile(i), 0, j)),
        ],
        out_specs=pl.BlockSpec((tm, tn), lambda i, j: (i, j)),
        out_shape=jax.ShapeDtypeStruct((m, n), F32),
        compiler_params=_params(blk + tmp, 2),
        name="out_proj",
    )(mix, w_out_b, x2, gt)


def kernel(x, c, ctx, c_ctx, norm_g, w_ada, b_ada, w_in, b_in, w_pool, s_pool, rpb,
           w_br_pool, w_br_attn, w_out, final_g):
    batch, seq, d = x.shape
    ctx_len = ctx.shape[1]
    depth = w_in.shape[0]
    ctx_row = batch

    x_lat = x.reshape(batch * seq, d)
    x_ctx = ctx.reshape(batch * ctx_len, d)
    cvec = jnp.zeros((8, d), F32).at[:batch].set(c).at[ctx_row].set(c_ctx)
    ada = _ada_call(cvec, w_ada, b_ada)
    c2 = _bias_pair_table(rpb)

    tm_in, tn_in = 2048, 512
    tm_out, tn_out = 1024, 1024
    tm_mrg, tn_mrg = 1024, 512
    tm_ctx = batch * ctx_len
    tn_ctx = 1024
    t_row = 256
    t_norm = 512
    lat_row = lambda tile_rows: (lambda i: i // (seq // tile_rows))
    ctx_row_fn = lambda i: ctx_row

    for l in range(depth):
        last = l == depth - 1
        mod = ada[l].reshape(8, 3, 1, d)
        sh, sc, gt = mod[:, 0], mod[:, 1], mod[:, 2]

        h_lat = _mod_call(x_lat, norm_g[l], sh, sc, lat_row(t_norm), t_norm)
        h_ctx = _mod_call(x_ctx, norm_g[l], sh, sc, ctx_row_fn, t_row)
        p_lat = _proj_call(h_lat, w_in, b_in, l, 0, D_IN, tm_in, tn_in)
        if last:
            p_ctx = _proj_call(h_ctx, w_in, b_in, l, COL_K, 2 * D_ATTN, tm_ctx, tn_in)
            kc_col, vc_col = 0, D_ATTN
        else:
            p_ctx = _proj_call(h_ctx, w_in, b_in, l, 0, D_IN, tm_ctx, tn_ctx)
            kc_col, vc_col = COL_K, COL_V

        a_attn, w_out_b = _na_call(p_lat, p_ctx, c2, w_out, l, batch, seq, ctx_len, kc_col, vc_col)
        a_pool = _pool_gate_call(p_lat, w_pool, s_pool[l], l, seq, t_row)
        mix = _merge_call(a_pool, a_attn, w_br_pool, w_br_attn, p_lat, l, tm_mrg, tn_mrg)
        x_lat_new = _out_call(mix, w_out_b, x_lat, gt, lat_row(tm_out), tm_out, tn_out)

        if not last:
            a_attn_c = _ctx_attn_call(p_ctx, batch, ctx_len)
            a_pool_c = _pool_gate_call(p_ctx, w_pool, s_pool[l], l, ctx_len, t_row)
            mix_c = _merge_call(a_pool_c, a_attn_c, w_br_pool, w_br_attn, p_ctx, l, tm_ctx, tn_mrg)
            x_ctx = _out_call(mix_c, w_out_b, x_ctx, gt, ctx_row_fn, tm_ctx, tn_out)
        x_lat = x_lat_new

    return _rms_call(x_lat, final_g, t_norm).reshape(batch, seq, d)
```

```python
import functools

import numpy as np
import jax
import jax.numpy as jnp
from jax import lax
from jax.experimental import pallas as pl
from jax.experimental.pallas import tpu as pltpu

F32 = jnp.float32
BF16 = jnp.bfloat16

D_MODEL = 4096
GRID_W = 64
LOG2_GRID_W = 6
D_POOL = D_MODEL // 2
POOL_WINDOWS = (2, 4, 8, 16)
D_POOL_GROUP = D_POOL // len(POOL_WINDOWS)
HEAD_DIM = 128
D_ATTN = D_MODEL // 2
N_HEADS = D_ATTN // HEAD_DIM
NA_KH = 8
NA_KW = 16
D_IN = 2 * D_POOL + 4 * D_ATTN + 2 * D_MODEL
RMS_EPS = 1e-6
NEG_INF = -1e30
ATTN_SCALE = HEAD_DIM ** -0.5
LOG2E = 1.4426950408889634

COL_U = 0
COL_ZP = D_POOL
COL_Q = 2 * D_POOL
COL_K = COL_Q + D_ATTN
COL_V = COL_K + D_ATTN
COL_ZA = COL_V + D_ATTN
COL_GP = COL_ZA + D_ATTN
COL_GA = COL_GP + D_MODEL

V7X_VMEM_LIMIT_BYTES = 60000 * 1024
COMPILER_SCRATCH_BYTES = 16 << 20
MATMUL_SPILL_BYTES = 4 << 20

QROWS = 4
BAND_ROWS = 12
TQ = QROWS * GRID_W
TK = BAND_ROWS * GRID_W
HEADS_PER_STEP = 4
CAST_ROWS, CAST_COLS = 128, 4096
POOL_HALO = 64
POOL_SUBTILE = 128


def _params(block_bytes, n_axes):
    return pltpu.CompilerParams(
        dimension_semantics=("arbitrary",) * n_axes,
        vmem_limit_bytes=int(min(V7X_VMEM_LIMIT_BYTES, block_bytes + COMPILER_SCRATCH_BYTES)),
    )


def _sigmoid(x):
    return 1.0 / (1.0 + jnp.exp(-x))


def _silu_bf16(z):
    hz = z * 0.5
    return hz + hz * jnp.tanh(hz)


def _split_bf16(v):
    hi = v.astype(BF16)
    lo = (v - hi.astype(F32)).astype(BF16)
    return hi, lo


def _ada_kernel(c_ref, w_ref, b_ref, o_ref):
    cv = c_ref[...]
    s_hi, s_lo = _split_bf16(cv * _sigmoid(cv))
    w_hi, w_lo = _split_bf16(w_ref[...])
    rows = s_hi.shape[0]
    r_hi = jnp.dot(jnp.concatenate([s_hi, s_lo], axis=0), w_hi, preferred_element_type=F32)
    r_lo = jnp.dot(s_hi, w_lo, preferred_element_type=F32)
    o_ref[...] = r_hi[:rows] + r_hi[rows:] + r_lo + b_ref[...]


def _ada_call(cvec, w_ada, b_ada):
    depth, d, n = w_ada.shape
    tn = 512
    blk = 2 * (d * tn * 4) + 2 * 8 * d * 4 + 4 * 8 * tn * 4
    return pl.pallas_call(
        _ada_kernel,
        grid=(depth, n // tn),
        in_specs=[
            pl.BlockSpec((8, d), lambda l, j: (0, 0)),
            pl.BlockSpec((None, d, tn), lambda l, j: (l, 0, j)),
            pl.BlockSpec((None, 1, tn), lambda l, j: (l, 0, j)),
        ],
        out_specs=pl.BlockSpec((None, 8, tn), lambda l, j: (l, 0, j)),
        out_shape=jax.ShapeDtypeStruct((depth, 8, n), F32),
        compiler_params=_params(blk, 2),
        name="ada",
    )(cvec, w_ada, b_ada.reshape(depth, 1, n))


def _mod_kernel(x_ref, g_ref, sh_ref, sc_ref, o_ref):
    x = x_ref[...]
    ms = jnp.mean(x * x, axis=-1, keepdims=True)
    y = x * lax.rsqrt(ms + RMS_EPS) * g_ref[...]
    o_ref[...] = (y * (1.0 + sc_ref[0]) + sh_ref[0]).astype(o_ref.dtype)


def _mod_call(x2, g, sh, sc, row_of_tile, tr):
    m, d = x2.shape
    vec = pl.BlockSpec((1, 1, d), lambda i: (row_of_tile(i), 0, 0))
    return pl.pallas_call(
        _mod_kernel,
        grid=(m // tr,),
        in_specs=[
            pl.BlockSpec((tr, d), lambda i: (i, 0)),
            pl.BlockSpec((1, d), lambda i: (0, 0)),
            vec, vec,
        ],
        out_specs=pl.BlockSpec((tr, d), lambda i: (i, 0)),
        out_shape=jax.ShapeDtypeStruct((m, d), BF16),
        compiler_params=_params(2 * tr * d * 6 + 3 * tr * d * 4, 1),
        name="modulate",
    )(x2, g.reshape(1, d), sh, sc)


def _rms_kernel(x_ref, g_ref, o_ref):
    x = x_ref[...]
    ms = jnp.mean(x * x, axis=-1, keepdims=True)
    o_ref[...] = x * lax.rsqrt(ms + RMS_EPS) * g_ref[...]


def _rms_call(x2, g, tr):
    m, d = x2.shape
    return pl.pallas_call(
        _rms_kernel,
        grid=(m // tr,),
        in_specs=[pl.BlockSpec((tr, d), lambda i: (i, 0)),
                  pl.BlockSpec((1, d), lambda i: (0, 0))],
        out_specs=pl.BlockSpec((tr, d), lambda i: (i, 0)),
        out_shape=jax.ShapeDtypeStruct((m, d), F32),
        compiler_params=_params(2 * tr * d * 8 + 2 * tr * d * 4, 1),
        name="final_norm",
    )(x2, g.reshape(1, d))


def _proj_kernel(a_ref, w_ref, b_ref, o_ref):
    acc = jnp.dot(a_ref[...], w_ref[...].astype(BF16), preferred_element_type=F32)
    o_ref[...] = (acc + b_ref[...]).astype(o_ref.dtype)


def _proj_call(a, w_stack, b_stack, layer, col0, n, tm, tn):
    m, k = a.shape
    depth, _, n_all = w_stack.shape
    jb = col0 // tn
    rows = tm * k * 2
    rest = 2 * (k * tn * 4 + tm * tn * 2 + tn * 4) + MATMUL_SPILL_BYTES
    row_buffers = 2 if 2 * rows + rest <= V7X_VMEM_LIMIT_BYTES else 1
    blk = row_buffers * rows + rest
    tmp = 0
    return pl.pallas_call(
        _proj_kernel,
        grid=(m // tm, n // tn),
        in_specs=[
            pl.BlockSpec((tm, k), lambda i, j: (i, 0), pipeline_mode=pl.Buffered(row_buffers)),
            pl.BlockSpec((None, k, tn), lambda i, j: (layer, 0, jb + j)),
            pl.BlockSpec((None, 1, tn), lambda i, j: (layer, 0, jb + j)),
        ],
        out_specs=pl.BlockSpec((tm, tn), lambda i, j: (i, j)),
        out_shape=jax.ShapeDtypeStruct((m, n), BF16),
        compiler_params=_params(blk + tmp, 2),
        name="in_proj",
    )(a, w_stack, b_stack.reshape(depth, 1, n_all))


def _dot_nt(a, b):
    return lax.dot_general(a, b, (((1,), (1,)), ((), ())), preferred_element_type=F32)


def _scaled_q(q_bf16):
    return (q_bf16.astype(F32) * (ATTN_SCALE * LOG2E)).astype(BF16)


def _with_ones(v):
    return jnp.concatenate([v, jnp.ones_like(v)], axis=1)


def _softmax_pv(scores, values, out_dtype):
    mx = functools.reduce(jnp.maximum, [jnp.max(s, axis=-1, keepdims=True) for s in scores])
    acc = functools.reduce(jnp.add, [jnp.dot(jnp.exp2(s - mx).astype(BF16), _with_ones(v), preferred_element_type=F32)
                                     for s, v in zip(scores, values)])
    return (acc[:, :HEAD_DIM] / acc[:, HEAD_DIM:]).astype(out_dtype)


def _band_start(g, n_rows):
    kr0 = jnp.clip(QROWS * g - NA_KH // 2, 0, n_rows - BAND_ROWS)
    return pl.multiple_of(kr0 * GRID_W, GRID_W)


def _na_scores(q_ref, k_ref, kc_ref, tab_ref, dst_ref, start):
    for hh in range(HEADS_PER_STEP):
        lanes = slice(hh * HEAD_DIM, (hh + 1) * HEAD_DIM)
        q = _scaled_q(q_ref[:, lanes])
        dst_ref[hh, :, :TK] = _dot_nt(q, k_ref[pl.ds(start, TK), lanes]) + tab_ref[hh]
        dst_ref[hh, :, TK:] = _dot_nt(q, kc_ref[:, lanes])


def _cast_tile_index(t, n_col_blocks, n_tiles):
    k = jnp.minimum(t, n_tiles - 1)
    return k // n_col_blocks, k % n_col_blocks


def _na_kernel(q_ref, k_ref, v_ref, kc_ref, vc_ref, za_ref, c2_ref, w_ref, o_ref, w_out_ref,
               tab_ref, sa_ref, sb_ref, *, n_groups, n_rows, n_cast_tiles):
    step_id = pl.program_id(2)
    t_lin = (pl.program_id(0) * pl.num_programs(1) + pl.program_id(1)) * pl.num_programs(2) + step_id

    @pl.when(t_lin < n_cast_tiles)
    def _():
        w_out_ref[...] = w_ref[...].astype(w_out_ref.dtype)

    last = n_groups - 1
    g = jnp.minimum(step_id, last)

    @pl.when(jnp.logical_or(g <= 1, g == last))
    def _():
        interior = jnp.logical_and(g > 0, g < last)
        lo_a = jnp.where(g == last, BAND_ROWS - NA_KH, 0)
        lo_b = jnp.where(interior, 1, 0)
        off = jnp.where(g == 0, NA_KH - 1,
                        jnp.where(g == last, NA_KH - 1 - BAND_ROWS + QROWS, NA_KH - 1 - NA_KH // 2))
        qrow = jnp.right_shift(lax.broadcasted_iota(jnp.int32, (TQ, 1), 0), LOG2_GRID_W)
        lo = lo_a + lo_b * qrow
        jrow = jnp.right_shift(lax.broadcasted_iota(jnp.int32, (1, TK), 1), LOG2_GRID_W)
        row_ok = jnp.logical_and(jrow >= lo, jrow < lo + NA_KH)
        for hh in range(HEADS_PER_STEP):
            bias = jnp.concatenate(
                [jnp.concatenate([c2_ref[hh, 2 * m - i + off + QROWS] for m in range(BAND_ROWS // 2)], axis=1)
                 for i in range(QROWS)], axis=0)
            tab_ref[hh] = jnp.where(row_ok, bias * LOG2E, NEG_INF)

    @pl.when(step_id == 0)
    def _():
        _na_scores(q_ref, k_ref, kc_ref, tab_ref, sb_ref, _band_start(0, n_rows))

    kstart = _band_start(g, n_rows)
    vstart = _band_start(jnp.maximum(step_id - 1, 0), n_rows)

    def step(src_ref, dst_ref):
        _na_scores(q_ref, k_ref, kc_ref, tab_ref, dst_ref, kstart)
        for hh in range(HEADS_PER_STEP):
            lanes = slice(hh * HEAD_DIM, (hh + 1) * HEAD_DIM)
            y = _softmax_pv((src_ref[hh, :, :TK], src_ref[hh, :, TK:]),
                            (v_ref[pl.ds(vstart, TK), lanes], vc_ref[:, lanes]), o_ref.dtype)
            o_ref[:, lanes] = y * _silu_bf16(za_ref[:, lanes])

    @pl.when(step_id % 2 == 0)
    def _():
        step(sb_ref, sa_ref)

    @pl.when(step_id % 2 == 1)
    def _():
        step(sa_ref, sb_ref)


def _na_call(p, pc, c2, w_out, layer, batch, seq, ctx_len, kc_col, vc_col):
    n_rows = seq // GRID_W
    n_groups = n_rows // QROWS
    last = n_groups - 1
    wb = HEADS_PER_STEP * HEAD_DIM
    n_c2 = c2.shape[2]
    c2_bytes = HEADS_PER_STEP * n_c2 * GRID_W * 2 * GRID_W * 4
    tab_bytes = HEADS_PER_STEP * TQ * TK * 4
    score_bytes = HEADS_PER_STEP * TQ * (TK + ctx_len) * 4
    blk = 2 * (TQ * wb * 2 * 3 + 2 * seq * wb * 2 + 2 * ctx_len * wb * 2 + c2_bytes
               + CAST_ROWS * CAST_COLS * 6)
    n_hg = N_HEADS // HEADS_PER_STEP
    steps = n_groups + 1
    n_cb = w_out.shape[2] // CAST_COLS
    n_cast_tiles = (w_out.shape[1] // CAST_ROWS) * n_cb
    assert n_cast_tiles <= batch * n_hg * steps

    def cast_tile(b, h, s):
        return _cast_tile_index((b * n_hg + h) * steps + s, n_cb, n_cast_tiles)

    kern = functools.partial(_na_kernel, n_groups=n_groups, n_rows=n_rows, n_cast_tiles=n_cast_tiles)
    return pl.pallas_call(
        kern,
        grid=(batch, N_HEADS // HEADS_PER_STEP, n_groups + 1),
        in_specs=[
            pl.BlockSpec((TQ, wb), lambda b, h, s: (b * n_groups + jnp.minimum(s, last), COL_Q // wb + h)),
            pl.BlockSpec((seq, wb), lambda b, h, s: (b, COL_K // wb + h)),
            pl.BlockSpec((seq, wb), lambda b, h, s: (b, COL_V // wb + h)),
            pl.BlockSpec((ctx_len, wb), lambda b, h, s: (b, kc_col // wb + h)),
            pl.BlockSpec((ctx_len, wb), lambda b, h, s: (b, vc_col // wb + h)),
            pl.BlockSpec((TQ, wb), lambda b, h, s: (b * n_groups + jnp.maximum(s - 1, 0), COL_ZA // wb + h)),
            pl.BlockSpec((None, HEADS_PER_STEP, n_c2, GRID_W, 2 * GRID_W), lambda b, h, s: (layer, h, 0, 0, 0)),
            pl.BlockSpec((None, CAST_ROWS, CAST_COLS), lambda b, h, s: (layer,) + cast_tile(b, h, s)),
        ],
        out_specs=[pl.BlockSpec((TQ, wb), lambda b, h, s: (b * n_groups + jnp.maximum(s - 1, 0), h)),
                   pl.BlockSpec((CAST_ROWS, CAST_COLS), cast_tile)],
        out_shape=[jax.ShapeDtypeStruct((batch * seq, D_ATTN), BF16),
                   jax.ShapeDtypeStruct(w_out.shape[1:], BF16)],
        scratch_shapes=[pltpu.VMEM((HEADS_PER_STEP, TQ, TK), F32),
                        pltpu.VMEM((HEADS_PER_STEP, TQ, TK + ctx_len), F32),
                        pltpu.VMEM((HEADS_PER_STEP, TQ, TK + ctx_len), F32)],
        compiler_params=_params(blk + tab_bytes + 2 * score_bytes, 3),
        name="na_attn",
    )(p, p, p, pc, pc, p, c2, w_out)


def _ctx_attn_kernel(q_ref, k_ref, v_ref, za_ref, o_ref):
    for hh in range(HEADS_PER_STEP):
        lanes = slice(hh * HEAD_DIM, (hh + 1) * HEAD_DIM)
        s = _dot_nt(_scaled_q(q_ref[:, lanes]), k_ref[:, lanes])
        o_ref[:, lanes] = _softmax_pv((s,), (v_ref[:, lanes],), o_ref.dtype) * _silu_bf16(za_ref[:, lanes])


def _ctx_attn_call(pc, batch, ctx_len):
    hb = HEADS_PER_STEP * HEAD_DIM
    return pl.pallas_call(
        _ctx_attn_kernel,
        grid=(batch, N_HEADS // HEADS_PER_STEP),
        in_specs=[
            pl.BlockSpec((ctx_len, hb), lambda b, h: (b, COL_Q // hb + h)),
            pl.BlockSpec((ctx_len, hb), lambda b, h: (b, COL_K // hb + h)),
            pl.BlockSpec((ctx_len, hb), lambda b, h: (b, COL_V // hb + h)),
            pl.BlockSpec((ctx_len, hb), lambda b, h: (b, COL_ZA // hb + h)),
        ],
        out_specs=pl.BlockSpec((ctx_len, hb), lambda b, h: (b, h)),
        out_shape=jax.ShapeDtypeStruct((batch * ctx_len, D_ATTN), BF16),
        compiler_params=_params(8 * ctx_len * hb * 2 + 8 * ctx_len * ctx_len * 4, 2),
        name="ctx_attn",
    )(pc, pc, pc, pc)


def _bias_pair_table(rpb):
    qc = np.arange(GRID_W)[:, None]
    kc = np.arange(GRID_W)[None, :]
    ws = np.clip(qc - NA_KW // 2, 0, GRID_W - NA_KW)
    col_ok = (kc >= ws) & (kc < ws + NA_KW)
    dc = kc - qc + NA_KW - 1
    n_dr, n_dc = 2 * NA_KH - 1, 2 * NA_KW - 1
    onehot = ((dc[None] == np.arange(n_dc)[:, None, None]) & col_ok[None]).astype(np.float32)
    n_e = BAND_ROWS + NA_KH - 1 + QROWS - 1
    pair_rows = np.stack([np.clip(np.arange(n_e) - QROWS, 0, n_dr - 1),
                          np.clip(np.arange(n_e) - QROWS + 1, 0, n_dr - 1)], axis=1)
    n_l, n_h = rpb.shape[:2]
    lhs = jnp.concatenate([rpb[:, :, pair_rows, :].reshape(n_l * n_h * n_e, 2 * n_dc),
                           jnp.ones((n_l * n_h * n_e, 1), F32)], axis=1)
    maps = np.zeros((2, n_dc, GRID_W, 2 * GRID_W), np.float32)
    maps[0, :, :, :GRID_W] = onehot
    maps[1, :, :, GRID_W:] = onehot
    mask_row = np.tile(np.where(col_ok, 0.0, NEG_INF).astype(np.float32), (1, 2))
    rhs = np.concatenate([maps.reshape(2 * n_dc, -1), mask_row.reshape(1, -1)], axis=0)
    table = jnp.dot(lhs, jnp.asarray(rhs), precision=lax.Precision.HIGHEST)
    return table.reshape(n_l, n_h, n_e, GRID_W, 2 * GRID_W)


def _pool_bands():
    r = np.arange(POOL_SUBTILE)[:, None]
    c = np.arange(POOL_SUBTILE + 2 * POOL_HALO)[None, :] - POOL_HALO
    return np.stack([(c >= r - w // 2) & (c <= r - w // 2 + w - 1) for w in POOL_WINDOWS]).astype(np.float32)


def _pool_gate_kernel(ucur_ref, uprev_ref, unext_ref, zp_ref, band_ref, wp_ref, sp_ref, o_ref,
                      *, tiles_per_seq, seq):
    t = ucur_ref.shape[0]
    cg = D_POOL_GROUP
    ts = pl.program_id(0) % tiles_per_seq
    base = ts * t
    uprev = jnp.where(ts > 0, uprev_ref[...], jnp.zeros_like(uprev_ref))
    unext = jnp.where(ts < tiles_per_seq - 1, unext_ref[...], jnp.zeros_like(unext_ref))
    ucat = jnp.concatenate([uprev, ucur_ref[...], unext], axis=0)
    tpos = lax.broadcasted_iota(jnp.int32, (t, 1), 0) + base
    for gi, w in enumerate(POOL_WINDOWS):
        cols = slice(gi * cg, (gi + 1) * cg)
        wsum = jnp.concatenate(
            [jnp.dot(band_ref[gi], ucat[r0:r0 + POOL_SUBTILE + 2 * POOL_HALO, cols], preferred_element_type=F32)
             for r0 in range(0, t, POOL_SUBTILE)], axis=0)
        cnt = (jnp.minimum(tpos - w // 2 + w - 1, seq - 1) - jnp.maximum(tpos - w // 2, 0) + 1).astype(F32)
        pooled = wsum / cnt - ucur_ref[:, cols].astype(F32)
        y = jnp.dot(pooled.astype(BF16), wp_ref[gi].astype(BF16), preferred_element_type=F32) * sp_ref[:, cols]
        o_ref[:, cols] = y.astype(BF16) * _silu_bf16(zp_ref[:, cols])


def _pool_gate_call(p, w_pool, s_pool_l, layer, seq, t):
    m = p.shape[0]
    tiles_per_seq = seq // t
    hpt = t // POOL_HALO
    n_halo_blocks = m // POOL_HALO
    wp_shape = w_pool.shape[1:]
    bands = jnp.asarray(_pool_bands(), BF16)
    blk = 2 * (3 * t * D_POOL * 2 + 2 * POOL_HALO * D_POOL * 2 + int(np.prod(wp_shape)) * 4 + D_POOL * 4
               + bands.size * 2)
    tmp = 8 * t * D_POOL_GROUP * 4
    kern = functools.partial(_pool_gate_kernel, tiles_per_seq=tiles_per_seq, seq=seq)
    return pl.pallas_call(
        kern,
        grid=(m // t,),
        in_specs=[
            pl.BlockSpec((t, D_POOL), lambda i: (i, COL_U // D_POOL)),
            pl.BlockSpec((POOL_HALO, D_POOL), lambda i: (jnp.maximum(i * hpt - 1, 0), COL_U // D_POOL)),
            pl.BlockSpec((POOL_HALO, D_POOL),
                         lambda i: (jnp.minimum((i + 1) * hpt, n_halo_blocks - 1), COL_U // D_POOL)),
            pl.BlockSpec((t, D_POOL), lambda i: (i, COL_ZP // D_POOL)),
            pl.BlockSpec(bands.shape, lambda i: (0, 0, 0)),
            pl.BlockSpec((None,) + wp_shape, lambda i: (layer, 0, 0, 0)),
            pl.BlockSpec((1, D_POOL), lambda i: (0, 0)),
        ],
        out_specs=pl.BlockSpec((t, D_POOL), lambda i: (i, 0)),
        out_shape=jax.ShapeDtypeStruct((m, D_POOL), BF16),
        compiler_params=_params(blk + tmp, 1),
        name="pool_gate",
    )(p, p, p, p, bands, w_pool, s_pool_l.reshape(1, D_POOL))


def _sigmoid_bf16(z):
    return 0.5 + 0.5 * jnp.tanh(z * 0.5)


def _merge_kernel(ap_ref, aa_ref, wp_ref, wa_ref, gp_ref, ga_ref, o_ref):
    br_p = jnp.dot(ap_ref[...], wp_ref[...].astype(BF16), preferred_element_type=F32)
    br_a = jnp.dot(aa_ref[...], wa_ref[...].astype(BF16), preferred_element_type=F32)
    gp = _sigmoid_bf16(gp_ref[...]).astype(F32)
    ga = _sigmoid_bf16(ga_ref[...]).astype(F32)
    o_ref[...] = (gp * br_p + ga * br_a).astype(o_ref.dtype)


def _merge_call(a_pool, a_attn, w_br_pool, w_br_attn, p, layer, tm, tn):
    m = a_pool.shape[0]
    n = D_MODEL
    blk = 2 * (tm * D_MODEL * 2 + 2 * D_POOL * tn * 4 + 3 * tm * tn * 2)
    tmp = 2 * tm * tn * 4
    return pl.pallas_call(
        _merge_kernel,
        grid=(m // tm, n // tn),
        in_specs=[
            pl.BlockSpec((tm, D_POOL), lambda i, j: (i, 0)),
            pl.BlockSpec((tm, D_ATTN), lambda i, j: (i, 0)),
            pl.BlockSpec((None, D_POOL, tn), lambda i, j: (layer, 0, j)),
            pl.BlockSpec((None, D_ATTN, tn), lambda i, j: (layer, 0, j)),
            pl.BlockSpec((tm, tn), lambda i, j: (i, COL_GP // tn + j)),
            pl.BlockSpec((tm, tn), lambda i, j: (i, COL_GA // tn + j)),
        ],
        out_specs=pl.BlockSpec((tm, tn), lambda i, j: (i, j)),
        out_shape=jax.ShapeDtypeStruct((m, n), BF16),
        compiler_params=_params(blk + tmp, 2),
        name="merge",
    )(a_pool, a_attn, w_br_pool, w_br_attn, p, p)


def _out_kernel(m_ref, w_ref, x_ref, gt_ref, o_ref):
    acc = jnp.dot(m_ref[...], w_ref[...], preferred_element_type=F32)
    o_ref[...] = x_ref[...] + gt_ref[0] * acc


def _out_call(mix, w_out_b, x2, gt, row_of_tile, tm, tn):
    m, k = mix.shape
    n = w_out_b.shape[1]
    blk = 2 * (tm * k * 2 + k * tn * 2 + 2 * tm * tn * 4 + tn * 4)
    tmp = tm * tn * 4
    return pl.pallas_call(
        _out_kernel,
        grid=(m // tm, n // tn),
        in_specs=[
            pl.BlockSpec((tm, k), lambda i, j: (i, 0)),
            pl.BlockSpec((k, tn), lambda i, j: (0, j)),
            pl.BlockSpec((tm, tn), lambda i, j: (i, j)),
            pl.BlockSpec((1, 1, tn), lambda i, j: (row_of_tile(i), 0, j)),
        ],
        out_specs=pl.BlockSpec((tm, tn), lambda i, j: (i, j)),
        out_shape=jax.ShapeDtypeStruct((m, n), F32),
        compiler_params=_params(blk + tmp, 2),
        name="out_proj",
    )(mix, w_out_b, x2, gt)


def kernel(x, c, ctx, c_ctx, norm_g, w_ada, b_ada, w_in, b_in, w_pool, s_pool, rpb,
           w_br_pool, w_br_attn, w_out, final_g):
    batch, seq, d = x.shape
    ctx_len = ctx.shape[1]
    depth = w_in.shape[0]
    ctx_row = batch

    x_lat = x.reshape(batch * seq, d)
    x_ctx = ctx.reshape(batch * ctx_len, d)
    cvec = jnp.zeros((8, d), F32).at[:batch].set(c).at[ctx_row].set(c_ctx)
    ada = _ada_call(cvec, w_ada, b_ada)
    c2 = _bias_pair_table(rpb)

    tm_in, tn_in = 2048, 512
    tm_out, tn_out = 1024, 1024
    tm_mrg, tn_mrg = 1024, 512
    tm_ctx = batch * ctx_len
    tn_ctx = 1024
    t_row = 256
    t_norm = 512
    lat_row = lambda tile_rows: (lambda i: i // (seq // tile_rows))
    ctx_row_fn = lambda i: ctx_row

    for l in range(depth):
        last = l == depth - 1
        mod = ada[l].reshape(8, 3, 1, d)
        sh, sc, gt = mod[:, 0], mod[:, 1], mod[:, 2]

        h_lat = _mod_call(x_lat, norm_g[l], sh, sc, lat_row(t_norm), t_norm)
        h_ctx = _mod_call(x_ctx, norm_g[l], sh, sc, ctx_row_fn, t_row)
        p_lat = _proj_call(h_lat, w_in, b_in, l, 0, D_IN, tm_in, tn_in)
        if last:
            p_ctx = _proj_call(h_ctx, w_in, b_in, l, COL_K, 2 * D_ATTN, tm_ctx, tn_in)
            kc_col, vc_col = 0, D_ATTN
        else:
            p_ctx = _proj_call(h_ctx, w_in, b_in, l, 0, D_IN, tm_ctx, tn_ctx)
            kc_col, vc_col = COL_K, COL_V

        a_attn, w_out_b = _na_call(p_lat, p_ctx, c2, w_out, l, batch, seq, ctx_len, kc_col, vc_col)
        a_pool = _pool_gate_call(p_lat, w_pool, s_pool[l], l, seq, t_norm)
        mix = _merge_call(a_pool, a_attn, w_br_pool, w_br_attn, p_lat, l, tm_mrg, tn_mrg)
        x_lat_new = _out_call(mix, w_out_b, x_lat, gt, lat_row(tm_out), tm_out, tn_out)

        if not last:
            a_attn_c = _ctx_attn_call(p_ctx, batch, ctx_len)
            a_pool_c = _pool_gate_call(p_ctx, w_pool, s_pool[l], l, ctx_len, t_row)
            mix_c = _merge_call(a_pool_c, a_attn_c, w_br_pool, w_br_attn, p_ctx, l, tm_ctx, tn_mrg)
            x_ctx = _out_call(mix_c, w_out_b, x_ctx, gt, ctx_row_fn, tm_ctx, tn_out)
        x_lat = x_lat_new

    return _rms_call(x_lat, final_g, t_norm).reshape(batch, seq, d)
```

```python
import functools

import numpy as np
import jax
import jax.numpy as jnp
from jax import lax
from jax.experimental import pallas as pl
from jax.experimental.pallas import tpu as pltpu

F32 = jnp.float32
BF16 = jnp.bfloat16

D_MODEL = 4096
GRID_W = 64
LOG2_GRID_W = 6
D_POOL = D_MODEL // 2
POOL_WINDOWS = (2, 4, 8, 16)
D_POOL_GROUP = D_POOL // len(POOL_WINDOWS)
HEAD_DIM = 128
D_ATTN = D_MODEL // 2
N_HEADS = D_ATTN // HEAD_DIM
NA_KH = 8
NA_KW = 16
D_IN = 2 * D_POOL + 4 * D_ATTN + 2 * D_MODEL
RMS_EPS = 1e-6
NEG_INF = -1e30
ATTN_SCALE = HEAD_DIM ** -0.5
LOG2E = 1.4426950408889634

COL_U = 0
COL_ZP = D_POOL
COL_Q = 2 * D_POOL
COL_K = COL_Q + D_ATTN
COL_V = COL_K + D_ATTN
COL_ZA = COL_V + D_ATTN
COL_GP = COL_ZA + D_ATTN
COL_GA = COL_GP + D_MODEL

V7X_VMEM_LIMIT_BYTES = 60000 * 1024
COMPILER_SCRATCH_BYTES = 16 << 20
MATMUL_SPILL_BYTES = 4 << 20

QROWS = 4
BAND_ROWS = 12
TQ = QROWS * GRID_W
TK = BAND_ROWS * GRID_W
HEADS_PER_STEP = 4
CAST_ROWS, CAST_COLS = 128, 4096
POOL_HALO = 64
POOL_SUBTILE = 128


def _params(block_bytes, n_axes):
    return pltpu.CompilerParams(
        dimension_semantics=("arbitrary",) * n_axes,
        vmem_limit_bytes=int(min(V7X_VMEM_LIMIT_BYTES, block_bytes + COMPILER_SCRATCH_BYTES)),
    )


def _sigmoid(x):
    return 1.0 / (1.0 + jnp.exp(-x))


def _silu_bf16(z):
    hz = z * 0.5
    return hz + hz * jnp.tanh(hz)


def _split_bf16(v):
    hi = v.astype(BF16)
    lo = (v - hi.astype(F32)).astype(BF16)
    return hi, lo


def _ada_kernel(c_ref, w_ref, b_ref, o_ref):
    cv = c_ref[...]
    s_hi, s_lo = _split_bf16(cv * _sigmoid(cv))
    w_hi, w_lo = _split_bf16(w_ref[...])
    rows = s_hi.shape[0]
    r_hi = jnp.dot(jnp.concatenate([s_hi, s_lo], axis=0), w_hi, preferred_element_type=F32)
    r_lo = jnp.dot(s_hi, w_lo, preferred_element_type=F32)
    o_ref[...] = r_hi[:rows] + r_hi[rows:] + r_lo + b_ref[...]


def _ada_call(cvec, w_ada, b_ada):
    depth, d, n = w_ada.shape
    tn = 512
    blk = 2 * (d * tn * 4) + 2 * 8 * d * 4 + 4 * 8 * tn * 4
    return pl.pallas_call(
        _ada_kernel,
        grid=(depth, n // tn),
        in_specs=[
            pl.BlockSpec((8, d), lambda l, j: (0, 0)),
            pl.BlockSpec((None, d, tn), lambda l, j: (l, 0, j)),
            pl.BlockSpec((None, 1, tn), lambda l, j: (l, 0, j)),
        ],
        out_specs=pl.BlockSpec((None, 8, tn), lambda l, j: (l, 0, j)),
        out_shape=jax.ShapeDtypeStruct((depth, 8, n), F32),
        compiler_params=_params(blk, 2),
        name="ada",
    )(cvec, w_ada, b_ada.reshape(depth, 1, n))


def _mod_kernel(x_ref, g_ref, sh_ref, sc_ref, o_ref):
    x = x_ref[...]
    ms = jnp.mean(x * x, axis=-1, keepdims=True)
    y = x * lax.rsqrt(ms + RMS_EPS) * g_ref[...]
    o_ref[...] = (y * (1.0 + sc_ref[0]) + sh_ref[0]).astype(o_ref.dtype)


def _mod_call(x2, g, sh, sc, row_of_tile, tr):
    m, d = x2.shape
    vec = pl.BlockSpec((1, 1, d), lambda i: (row_of_tile(i), 0, 0))
    return pl.pallas_call(
        _mod_kernel,
        grid=(m // tr,),
        in_specs=[
            pl.BlockSpec((tr, d), lambda i: (i, 0)),
            pl.BlockSpec((1, d), lambda i: (0, 0)),
            vec, vec,
        ],
        out_specs=pl.BlockSpec((tr, d), lambda i: (i, 0)),
        out_shape=jax.ShapeDtypeStruct((m, d), BF16),
        compiler_params=_params(2 * tr * d * 6 + 3 * tr * d * 4, 1),
        name="modulate",
    )(x2, g.reshape(1, d), sh, sc)


def _rms_kernel(x_ref, g_ref, o_ref):
    x = x_ref[...]
    ms = jnp.mean(x * x, axis=-1, keepdims=True)
    o_ref[...] = x * lax.rsqrt(ms + RMS_EPS) * g_ref[...]


def _rms_call(x2, g, tr):
    m, d = x2.shape
    return pl.pallas_call(
        _rms_kernel,
        grid=(m // tr,),
        in_specs=[pl.BlockSpec((tr, d), lambda i: (i, 0)),
                  pl.BlockSpec((1, d), lambda i: (0, 0))],
        out_specs=pl.BlockSpec((tr, d), lambda i: (i, 0)),
        out_shape=jax.ShapeDtypeStruct((m, d), F32),
        compiler_params=_params(2 * tr * d * 8 + 2 * tr * d * 4, 1),
        name="final_norm",
    )(x2, g.reshape(1, d))


def _proj_kernel(a_ref, w_ref, b_ref, o_ref):
    acc = jnp.dot(a_ref[...], w_ref[...].astype(BF16), preferred_element_type=F32)
    o_ref[...] = (acc + b_ref[...]).astype(o_ref.dtype)


def _proj_call(a, w_stack, b_stack, layer, col0, n, tm, tn):
    m, k = a.shape
    depth, _, n_all = w_stack.shape
    jb = col0 // tn
    rows = tm * k * 2
    rest = 2 * (k * tn * 4 + tm * tn * 2 + tn * 4) + MATMUL_SPILL_BYTES
    row_buffers = 2 if 2 * rows + rest <= V7X_VMEM_LIMIT_BYTES else 1
    blk = row_buffers * rows + rest
    tmp = 0
    return pl.pallas_call(
        _proj_kernel,
        grid=(m // tm, n // tn),
        in_specs=[
            pl.BlockSpec((tm, k), lambda i, j: (i, 0), pipeline_mode=pl.Buffered(row_buffers)),
            pl.BlockSpec((None, k, tn), lambda i, j: (layer, 0, jb + j)),
            pl.BlockSpec((None, 1, tn), lambda i, j: (layer, 0, jb + j)),
        ],
        out_specs=pl.BlockSpec((tm, tn), lambda i, j: (i, j)),
        out_shape=jax.ShapeDtypeStruct((m, n), BF16),
        compiler_params=_params(blk + tmp, 2),
        name="in_proj",
    )(a, w_stack, b_stack.reshape(depth, 1, n_all))


def _dot_nt(a, b):
    return lax.dot_general(a, b, (((1,), (1,)), ((), ())), preferred_element_type=F32)


def _scaled_q(q_bf16):
    return (q_bf16.astype(F32) * (ATTN_SCALE * LOG2E)).astype(BF16)


def _with_ones(v):
    return jnp.concatenate([v, jnp.ones_like(v)], axis=1)


def _softmax_pv(scores, values, out_dtype):
    mx = functools.reduce(jnp.maximum, [jnp.max(s, axis=-1, keepdims=True) for s in scores])
    acc = functools.reduce(jnp.add, [jnp.dot(jnp.exp2(s - mx).astype(BF16), _with_ones(v), preferred_element_type=F32)
                                     for s, v in zip(scores, values)])
    return (acc[:, :HEAD_DIM] / acc[:, HEAD_DIM:]).astype(out_dtype)


def _band_start(g, n_rows):
    kr0 = jnp.clip(QROWS * g - NA_KH // 2, 0, n_rows - BAND_ROWS)
    return pl.multiple_of(kr0 * GRID_W, GRID_W)


def _na_scores(q_ref, k_ref, kc_ref, tab_ref, dst_ref, start):
    for hh in range(HEADS_PER_STEP):
        lanes = slice(hh * HEAD_DIM, (hh + 1) * HEAD_DIM)
        q = _scaled_q(q_ref[:, lanes])
        dst_ref[hh, :, :TK] = _dot_nt(q, k_ref[pl.ds(start, TK), lanes]) + tab_ref[hh]
        dst_ref[hh, :, TK:] = _dot_nt(q, kc_ref[:, lanes])


def _cast_tile_index(t, n_col_blocks, n_tiles):
    k = jnp.minimum(t, n_tiles - 1)
    return k // n_col_blocks, k % n_col_blocks


def _na_kernel(q_ref, k_ref, v_ref, kc_ref, vc_ref, za_ref, c2_ref, w_ref, o_ref, w_out_ref,
               tab_ref, sa_ref, sb_ref, *, n_groups, n_rows, n_cast_tiles):
    step_id = pl.program_id(2)
    t_lin = (pl.program_id(0) * pl.num_programs(1) + pl.program_id(1)) * pl.num_programs(2) + step_id

    @pl.when(t_lin < n_cast_tiles)
    def _():
        w_out_ref[...] = w_ref[...].astype(w_out_ref.dtype)

    last = n_groups - 1
    g = jnp.minimum(step_id, last)

    @pl.when(jnp.logical_or(g <= 1, g == last))
    def _():
        interior = jnp.logical_and(g > 0, g < last)
        lo_a = jnp.where(g == last, BAND_ROWS - NA_KH, 0)
        lo_b = jnp.where(interior, 1, 0)
        off = jnp.where(g == 0, NA_KH - 1,
                        jnp.where(g == last, NA_KH - 1 - BAND_ROWS + QROWS, NA_KH - 1 - NA_KH // 2))
        qrow = jnp.right_shift(lax.broadcasted_iota(jnp.int32, (TQ, 1), 0), LOG2_GRID_W)
        lo = lo_a + lo_b * qrow
        jrow = jnp.right_shift(lax.broadcasted_iota(jnp.int32, (1, TK), 1), LOG2_GRID_W)
        row_ok = jnp.logical_and(jrow >= lo, jrow < lo + NA_KH)
        for hh in range(HEADS_PER_STEP):
            bias = jnp.concatenate(
                [jnp.concatenate([c2_ref[hh, 2 * m - i + off + QROWS] for m in range(BAND_ROWS // 2)], axis=1)
                 for i in range(QROWS)], axis=0)
            tab_ref[hh] = jnp.where(row_ok, bias * LOG2E, NEG_INF)

    @pl.when(step_id == 0)
    def _():
        _na_scores(q_ref, k_ref, kc_ref, tab_ref, sb_ref, _band_start(0, n_rows))

    kstart = _band_start(g, n_rows)
    vstart = _band_start(jnp.maximum(step_id - 1, 0), n_rows)

    def step(src_ref, dst_ref):
        _na_scores(q_ref, k_ref, kc_ref, tab_ref, dst_ref, kstart)
        for hh in range(HEADS_PER_STEP):
            lanes = slice(hh * HEAD_DIM, (hh + 1) * HEAD_DIM)
            y = _softmax_pv((src_ref[hh, :, :TK], src_ref[hh, :, TK:]),
                            (v_ref[pl.ds(vstart, TK), lanes], vc_ref[:, lanes]), o_ref.dtype)
            o_ref[:, lanes] = y * _silu_bf16(za_ref[:, lanes])

    @pl.when(step_id % 2 == 0)
    def _():
        step(sb_ref, sa_ref)

    @pl.when(step_id % 2 == 1)
    def _():
        step(sa_ref, sb_ref)


def _na_call(p, pc, c2, w_out, layer, batch, seq, ctx_len, kc_col, vc_col):
    n_rows = seq // GRID_W
    n_groups = n_rows // QROWS
    last = n_groups - 1
    wb = HEADS_PER_STEP * HEAD_DIM
    n_c2 = c2.shape[2]
    c2_bytes = HEADS_PER_STEP * n_c2 * GRID_W * 2 * GRID_W * 4
    tab_bytes = HEADS_PER_STEP * TQ * TK * 4
    score_bytes = HEADS_PER_STEP * TQ * (TK + ctx_len) * 4
    blk = 2 * (TQ * wb * 2 * 3 + 2 * seq * wb * 2 + 2 * ctx_len * wb * 2 + c2_bytes
               + CAST_ROWS * CAST_COLS * 6)
    n_hg = N_HEADS // HEADS_PER_STEP
    steps = n_groups + 1
    n_cb = w_out.shape[2] // CAST_COLS
    n_cast_tiles = (w_out.shape[1] // CAST_ROWS) * n_cb
    assert n_cast_tiles <= batch * n_hg * steps

    def cast_tile(b, h, s):
        return _cast_tile_index((b * n_hg + h) * steps + s, n_cb, n_cast_tiles)

    kern = functools.partial(_na_kernel, n_groups=n_groups, n_rows=n_rows, n_cast_tiles=n_cast_tiles)
    return pl.pallas_call(
        kern,
        grid=(batch, N_HEADS // HEADS_PER_STEP, n_groups + 1),
        in_specs=[
            pl.BlockSpec((TQ, wb), lambda b, h, s: (b * n_groups + jnp.minimum(s, last), COL_Q // wb + h)),
            pl.BlockSpec((seq, wb), lambda b, h, s: (b, COL_K // wb + h)),
            pl.BlockSpec((seq, wb), lambda b, h, s: (b, COL_V // wb + h)),
            pl.BlockSpec((ctx_len, wb), lambda b, h, s: (b, kc_col // wb + h)),
            pl.BlockSpec((ctx_len, wb), lambda b, h, s: (b, vc_col // wb + h)),
            pl.BlockSpec((TQ, wb), lambda b, h, s: (b * n_groups + jnp.maximum(s - 1, 0), COL_ZA // wb + h)),
            pl.BlockSpec((None, HEADS_PER_STEP, n_c2, GRID_W, 2 * GRID_W), lambda b, h, s: (layer, h, 0, 0, 0)),
            pl.BlockSpec((None, CAST_ROWS, CAST_COLS), lambda b, h, s: (layer,) + cast_tile(b, h, s)),
        ],
        out_specs=[pl.BlockSpec((TQ, wb), lambda b, h, s: (b * n_groups + jnp.maximum(s - 1, 0), h)),
                   pl.BlockSpec((CAST_ROWS, CAST_COLS), cast_tile)],
        out_shape=[jax.ShapeDtypeStruct((batch * seq, D_ATTN), BF16),
                   jax.ShapeDtypeStruct(w_out.shape[1:], BF16)],
        scratch_shapes=[pltpu.VMEM((HEADS_PER_STEP, TQ, TK), F32),
                        pltpu.VMEM((HEADS_PER_STEP, TQ, TK + ctx_len), F32),
                        pltpu.VMEM((HEADS_PER_STEP, TQ, TK + ctx_len), F32)],
        compiler_params=_params(blk + tab_bytes + 2 * score_bytes, 3),
        name="na_attn",
    )(p, p, p, pc, pc, p, c2, w_out)


def _ctx_attn_kernel(q_ref, k_ref, v_ref, za_ref, o_ref):
    for hh in range(HEADS_PER_STEP):
        lanes = slice(hh * HEAD_DIM, (hh + 1) * HEAD_DIM)
        s = _dot_nt(_scaled_q(q_ref[:, lanes]), k_ref[:, lanes])
        o_ref[:, lanes] = _softmax_pv((s,), (v_ref[:, lanes],), o_ref.dtype) * _silu_bf16(za_ref[:, lanes])


def _ctx_attn_call(pc, batch, ctx_len):
    hb = HEADS_PER_STEP * HEAD_DIM
    return pl.pallas_call(
        _ctx_attn_kernel,
        grid=(batch, N_HEADS // HEADS_PER_STEP),
        in_specs=[
            pl.BlockSpec((ctx_len, hb), lambda b, h: (b, COL_Q // hb + h)),
            pl.BlockSpec((ctx_len, hb), lambda b, h: (b, COL_K // hb + h)),
            pl.BlockSpec((ctx_len, hb), lambda b, h: (b, COL_V // hb + h)),
            pl.BlockSpec((ctx_len, hb), lambda b, h: (b, COL_ZA // hb + h)),
        ],
        out_specs=pl.BlockSpec((ctx_len, hb), lambda b, h: (b, h)),
        out_shape=jax.ShapeDtypeStruct((batch * ctx_len, D_ATTN), BF16),
        compiler_params=_params(8 * ctx_len * hb * 2 + 8 * ctx_len * ctx_len * 4, 2),
        name="ctx_attn",
    )(pc, pc, pc, pc)


def _bias_pair_table(rpb):
    qc = np.arange(GRID_W)[:, None]
    kc = np.arange(GRID_W)[None, :]
    ws = np.clip(qc - NA_KW // 2, 0, GRID_W - NA_KW)
    col_ok = (kc >= ws) & (kc < ws + NA_KW)
    dc = kc - qc + NA_KW - 1
    n_dr, n_dc = 2 * NA_KH - 1, 2 * NA_KW - 1
    onehot = ((dc[None] == np.arange(n_dc)[:, None, None]) & col_ok[None]).astype(np.float32)
    n_e = BAND_ROWS + NA_KH - 1 + QROWS - 1
    pair_rows = np.stack([np.clip(np.arange(n_e) - QROWS, 0, n_dr - 1),
                          np.clip(np.arange(n_e) - QROWS + 1, 0, n_dr - 1)], axis=1)
    n_l, n_h = rpb.shape[:2]
    lhs = jnp.concatenate([rpb[:, :, pair_rows, :].reshape(n_l * n_h * n_e, 2 * n_dc),
                           jnp.ones((n_l * n_h * n_e, 1), F32)], axis=1)
    maps = np.zeros((2, n_dc, GRID_W, 2 * GRID_W), np.float32)
    maps[0, :, :, :GRID_W] = onehot
    maps[1, :, :, GRID_W:] = onehot
    mask_row = np.tile(np.where(col_ok, 0.0, NEG_INF).astype(np.float32), (1, 2))
    rhs = np.concatenate([maps.reshape(2 * n_dc, -1), mask_row.reshape(1, -1)], axis=0)
    table = jnp.dot(lhs, jnp.asarray(rhs), precision=lax.Precision.HIGHEST)
    return table.reshape(n_l, n_h, n_e, GRID_W, 2 * GRID_W)


def _pool_bands():
    r = np.arange(POOL_SUBTILE)[:, None]
    c = np.arange(POOL_SUBTILE + 2 * POOL_HALO)[None, :] - POOL_HALO
    return np.stack([(c >= r - w // 2) & (c <= r - w // 2 + w - 1) for w in POOL_WINDOWS]).astype(np.float32)


def _pool_gate_kernel(ucur_ref, uprev_ref, unext_ref, zp_ref, band_ref, wp_ref, sp_ref, o_ref,
                      *, tiles_per_seq, seq):
    t = ucur_ref.shape[0]
    cg = D_POOL_GROUP
    ts = pl.program_id(0) % tiles_per_seq
    base = ts * t
    uprev = jnp.where(ts > 0, uprev_ref[...], jnp.zeros_like(uprev_ref))
    unext = jnp.where(ts < tiles_per_seq - 1, unext_ref[...], jnp.zeros_like(unext_ref))
    ucat = jnp.concatenate([uprev, ucur_ref[...], unext], axis=0)
    tpos = lax.broadcasted_iota(jnp.int32, (t, 1), 0) + base
    for gi, w in enumerate(POOL_WINDOWS):
        cols = slice(gi * cg, (gi + 1) * cg)
        wsum = jnp.concatenate(
            [jnp.dot(band_ref[gi], ucat[r0:r0 + POOL_SUBTILE + 2 * POOL_HALO, cols], preferred_element_type=F32)
             for r0 in range(0, t, POOL_SUBTILE)], axis=0)
        cnt = (jnp.minimum(tpos - w // 2 + w - 1, seq - 1) - jnp.maximum(tpos - w // 2, 0) + 1).astype(F32)
        pooled = wsum / cnt - ucur_ref[:, cols].astype(F32)
        y = jnp.dot(pooled.astype(BF16), wp_ref[gi].astype(BF16), preferred_element_type=F32) * sp_ref[:, cols]
        o_ref[:, cols] = y.astype(BF16) * _silu_bf16(zp_ref[:, cols])


def _pool_gate_call(p, w_pool, s_pool_l, layer, seq, t):
    m = p.shape[0]
    tiles_per_seq = seq // t
    hpt = t // POOL_HALO
    n_halo_blocks = m // POOL_HALO
    wp_shape = w_pool.shape[1:]
    bands = jnp.asarray(_pool_bands(), BF16)
    blk = 2 * (3 * t * D_POOL * 2 + 2 * POOL_HALO * D_POOL * 2 + int(np.prod(wp_shape)) * 4 + D_POOL * 4
               + bands.size * 2)
    tmp = 8 * t * D_POOL_GROUP * 4
    kern = functools.partial(_pool_gate_kernel, tiles_per_seq=tiles_per_seq, seq=seq)
    return pl.pallas_call(
        kern,
        grid=(m // t,),
        in_specs=[
            pl.BlockSpec((t, D_POOL), lambda i: (i, COL_U // D_POOL)),
            pl.BlockSpec((POOL_HALO, D_POOL), lambda i: (jnp.maximum(i * hpt - 1, 0), COL_U // D_POOL)),
            pl.BlockSpec((POOL_HALO, D_POOL),
                         lambda i: (jnp.minimum((i + 1) * hpt, n_halo_blocks - 1), COL_U // D_POOL)),
            pl.BlockSpec((t, D_POOL), lambda i: (i, COL_ZP // D_POOL)),
            pl.BlockSpec(bands.shape, lambda i: (0, 0, 0)),
            pl.BlockSpec((None,) + wp_shape, lambda i: (layer, 0, 0, 0)),
            pl.BlockSpec((1, D_POOL), lambda i: (0, 0)),
        ],
        out_specs=pl.BlockSpec((t, D_POOL), lambda i: (i, 0)),
        out_shape=jax.ShapeDtypeStruct((m, D_POOL), BF16),
        compiler_params=_params(blk + tmp, 1),
        name="pool_gate",
    )(p, p, p, p, bands, w_pool, s_pool_l.reshape(1, D_POOL))


def _sigmoid_bf16(z):
    return 0.5 + 0.5 * jnp.tanh(z * 0.5)


def _merge_kernel(ap_ref, aa_ref, wp_ref, wa_ref, gp_ref, ga_ref, o_ref):
    br_p = jnp.dot(ap_ref[...], wp_ref[...].astype(BF16), preferred_element_type=F32)
    br_a = jnp.dot(aa_ref[...], wa_ref[...].astype(BF16), preferred_element_type=F32)
    gp = _sigmoid_bf16(gp_ref[...]).astype(F32)
    ga = _sigmoid_bf16(ga_ref[...]).astype(F32)
    o_ref[...] = (gp * br_p + ga * br_a).astype(o_ref.dtype)


def _merge_call(a_pool, a_attn, w_br_pool, w_br_attn, p, layer, tm, tn):
    m = a_pool.shape[0]
    n = D_MODEL
    blk = 2 * (tm * D_MODEL * 2 + 2 * D_POOL * tn * 4 + 3 * tm * tn * 2)
    tmp = 2 * tm * tn * 4
    return pl.pallas_call(
        _merge_kernel,
        grid=(m // tm, n // tn),
        in_specs=[
            pl.BlockSpec((tm, D_POOL), lambda i, j: (i, 0)),
            pl.BlockSpec((tm, D_ATTN), lambda i, j: (i, 0)),
            pl.BlockSpec((None, D_POOL, tn), lambda i, j: (layer, 0, j)),
            pl.BlockSpec((None, D_ATTN, tn), lambda i, j: (layer, 0, j)),
            pl.BlockSpec((tm, tn), lambda i, j: (i, COL_GP // tn + j)),
            pl.BlockSpec((tm, tn), lambda i, j: (i, COL_GA // tn + j)),
        ],
        out_specs=pl.BlockSpec((tm, tn), lambda i, j: (i, j)),
        out_shape=jax.ShapeDtypeStruct((m, n), BF16),
        compiler_params=_params(blk + tmp, 2),
        name="merge",
    )(a_pool, a_attn, w_br_pool, w_br_attn, p, p)


def _out_kernel(m_ref, w_ref, x_ref, gt_ref, o_ref):
    acc = jnp.dot(m_ref[...], w_ref[...], preferred_element_type=F32)
    o_ref[...] = x_ref[...] + gt_ref[0] * acc


def _out_call(mix, w_out_b, x2, gt, row_of_tile, tm, tn):
    m, k = mix.shape
    n = w_out_b.shape[1]
    blk = 2 * (tm * k * 2 + k * tn * 2 + 2 * tm * tn * 4 + tn * 4)
    tmp = tm * tn * 4
    return pl.pallas_call(
        _out_kernel,
        grid=(m // tm, n // tn),
        in_specs=[
            pl.BlockSpec((tm, k), lambda i, j: (i, 0)),
            pl.BlockSpec((k, tn), lambda i, j: (0, j)),
            pl.BlockSpec((tm, tn), lambda i, j: (i, j)),
            pl.BlockSpec((1, 1, tn), lambda i, j: (row_of_tile(i), 0, j)),
        ],
        out_specs=pl.BlockSpec((tm, tn), lambda i, j: (i, j)),
        out_shape=jax.ShapeDtypeStruct((m, n), F32),
        compiler_params=_params(blk + tmp, 2),
        name="out_proj",
    )(mix, w_out_b, x2, gt)


def kernel(x, c, ctx, c_ctx, norm_g, w_ada, b_ada, w_in, b_in, w_pool, s_pool, rpb,
           w_br_pool, w_br_attn, w_out, final_g):
    batch, seq, d = x.shape
    ctx_len = ctx.shape[1]
    depth = w_in.shape[0]
    ctx_row = batch

    x_lat = x.reshape(batch * seq, d)
    x_ctx = ctx.reshape(batch * ctx_len, d)
    cvec = jnp.zeros((8, d), F32).at[:batch].set(c).at[ctx_row].set(c_ctx)
    ada = _ada_call(cvec, w_ada, b_ada)
    c2 = _bias_pair_table(rpb)

    tm_in, tn_in = 2048, 512
    tm_out, tn_out = 1024, 1024
    tm_mrg, tn_mrg = 1024, 512
    tm_ctx = batch * ctx_len
    tn_ctx = 1024
    t_row = 256
    t_norm = 512
    t_pool = 1024
    lat_row = lambda tile_rows: (lambda i: i // (seq // tile_rows))
    ctx_row_fn = lambda i: ctx_row

    for l in range(depth):
        last = l == depth - 1
        mod = ada[l].reshape(8, 3, 1, d)
        sh, sc, gt = mod[:, 0], mod[:, 1], mod[:, 2]

        h_lat = _mod_call(x_lat, norm_g[l], sh, sc, lat_row(t_norm), t_norm)
        h_ctx = _mod_call(x_ctx, norm_g[l], sh, sc, ctx_row_fn, t_row)
        p_lat = _proj_call(h_lat, w_in, b_in, l, 0, D_IN, tm_in, tn_in)
        if last:
            p_ctx = _proj_call(h_ctx, w_in, b_in, l, COL_K, 2 * D_ATTN, tm_ctx, tn_in)
            kc_col, vc_col = 0, D_ATTN
        else:
            p_ctx = _proj_call(h_ctx, w_in, b_in, l, 0, D_IN, tm_ctx, tn_ctx)
            kc_col, vc_col = COL_K, COL_V

        a_attn, w_out_b = _na_call(p_lat, p_ctx, c2, w_out, l, batch, seq, ctx_len, kc_col, vc_col)
        a_pool = _pool_gate_call(p_lat, w_pool, s_pool[l], l, seq, t_pool)
        mix = _merge_call(a_pool, a_attn, w_br_pool, w_br_attn, p_lat, l, tm_mrg, tn_mrg)
        x_lat_new = _out_call(mix, w_out_b, x_lat, gt, lat_row(tm_out), tm_out, tn_out)

        if not last:
            a_attn_c = _ctx_attn_call(p_ctx, batch, ctx_len)
            a_pool_c = _pool_gate_call(p_ctx, w_pool, s_pool[l], l, ctx_len, t_row)
            mix_c = _merge_call(a_pool_c, a_attn_c, w_br_pool, w_br_attn, p_ctx, l, tm_ctx, tn_mrg)
            x_ctx = _out_call(mix_c, w_out_b, x_ctx, gt, ctx_row_fn, tm_ctx, tn_out)
        x_lat = x_lat_new

    return _rms_call(x_lat, final_g, t_norm).reshape(batch, seq, d)
```
